```python
import math
import jax, jax.numpy as jnp
from jax import lax
import numpy as np

D_MODEL = 2048
BATCH = 2
SEQ = 4096
DEPTH = 2
DEC_BATCH = 8
DEC_SEQ = 1
PAST_LEN = 16384
PAGE_SIZE = 128

N_MIXERS = 2
N_A_LAYERS = (DEPTH + 1) // 2
N_B_LAYERS = DEPTH // 2
CHUNK = 128
D_A = 3 * D_MODEL
N_GROUPS_A = 16
GROUP_DIM_A = D_A // N_GROUPS_A
N_HEADS = 16
HEAD_DIM = D_MODEL // N_HEADS
MOBA_BLOCK = 256
MOBA_TOPK = 3
MOBA_QBLOCK = 32
N_EXPERTS = 32
TOP_K = 4
D_EXPERT = D_MODEL
SWIGLU_LIMIT = 7.0
SWIGLU_ALPHA = 1.702
DEEPNORM_ALPHA = (2 * DEPTH) ** 0.25
DEEPNORM_BETA = (8 * DEPTH) ** -0.25
LN_EPS = 1e-5

kernel_name = 'hybrid_gmlp_moba_moe_decode_step'


def layer_norm(x, g, b):
    xf = x.astype(jnp.float32)
    mu = jnp.mean(xf, axis=-1, keepdims=True)
    var = jnp.mean(jnp.square(xf - mu), axis=-1, keepdims=True)
    return ((xf - mu) * lax.rsqrt(var + LN_EPS) * g + b).astype(x.dtype)


def chunk_mlp(x, w_in, b_in, lnv_g, lnv_b, w_s, b_s, w_out):
    bsz, seq_len, _ = x.shape
    z = jax.nn.gelu(x @ w_in + b_in, approximate=False)
    u, v = jnp.split(z, 2, axis=-1)
    v = layer_norm(v, lnv_g, lnv_b)
    n_chunks = -(-seq_len // CHUNK)
    pad = n_chunks * CHUNK - seq_len
    vp = jnp.pad(v, ((0, 0), (0, pad), (0, 0))).reshape(bsz, n_chunks, CHUNK, N_GROUPS_A, GROUP_DIM_A)
    causal = jnp.tril(jnp.ones((CHUNK, CHUNK), dtype=bool))
    ws = jnp.where(causal[None], w_s, 0.0)
    mixed = jnp.einsum('gts,bcsgd->bctgd', ws, vp) + b_s.T[None, None, :, :, None]
    mixed = mixed.reshape(bsz, n_chunks * CHUNK, D_A)[:, :seq_len]
    return (u * mixed) @ w_out, v


def alibi_slopes():
    return 2.0 ** (-8.0 * jnp.arange(1, N_HEADS + 1, dtype=jnp.float32) / N_HEADS)


def qkv_proj(x, w_qkv):
    bsz, seq_len, _ = x.shape
    qkv = (x @ w_qkv).reshape(bsz, seq_len, 3, N_HEADS, HEAD_DIM)
    return qkv[:, :, 0], qkv[:, :, 1], qkv[:, :, 2]


def moba_attention(q, k, v, q_pos0, q_block):
    bsz, lq = q.shape[:2]
    lk = k.shape[1]
    nb = -(-lk // MOBA_BLOCK)
    pad = nb * MOBA_BLOCK - lk
    kp = jnp.pad(k, ((0, 0), (0, pad), (0, 0), (0, 0)))
    vp = jnp.pad(v, ((0, 0), (0, pad), (0, 0), (0, 0)))
    kb = kp.reshape(bsz, nb, MOBA_BLOCK, N_HEADS, HEAD_DIM).transpose(0, 3, 1, 2, 4)
    vb = vp.reshape(bsz, nb, MOBA_BLOCK, N_HEADS, HEAD_DIM).transpose(0, 3, 1, 2, 4)
    k_mean = jnp.mean(kb, axis=3, dtype=jnp.float32)
    n_sel = min(MOBA_TOPK, nb)
    slopes = alibi_slopes()
    scale = HEAD_DIM ** -0.5
    b_idx = jnp.arange(bsz)[:, None, None, None]
    h_idx = jnp.arange(N_HEADS)[None, :, None, None]
    blk_range = jnp.arange(nb)
    slot = jnp.arange(MOBA_BLOCK)
    rank5 = jnp.arange(n_sel + 1)[None, None, None, :, None]

    def attend_block(args):
        qc, pos = args
        own = pos // MOBA_BLOCK
        gate = jnp.einsum('bqhd,bhnd->bhqn', qc.astype(jnp.float32), k_mean)
        gate = jnp.where(blk_range[None, None, None, :] < own[None, None, :, None], gate, -jnp.inf)
        _, top_idx = lax.top_k(gate, n_sel)
        own_idx = jnp.broadcast_to(own[None, None, :, None], top_idx.shape[:3] + (1,))
        sel = jnp.concatenate([top_idx, own_idx], axis=-1)
        kg = kb[b_idx, h_idx, sel]
        vg = vb[b_idx, h_idx, sel]
        s = jnp.einsum('bqhd,bhqrkd->bhqrk', qc, kg).astype(jnp.float32) * scale
        k_pos = sel[..., None] * MOBA_BLOCK + slot
        pos5 = pos[None, None, :, None, None]
        own5 = own[None, None, :, None, None]
        s = s - slopes[None, :, None, None, None] * jnp.abs(pos5 - k_pos).astype(jnp.float32)
        valid = jnp.where(rank5 == n_sel, k_pos <= pos5, rank5 < own5)
        s = jnp.where(valid, s, -jnp.inf)
        p = jax.nn.softmax(s.reshape(s.shape[:3] + (-1,)), axis=-1).reshape(s.shape)
        return jnp.einsum('bhqrk,bhqrkd->bqhd', p.astype(vg.dtype), vg)

    n_qc = lq // q_block
    qc = q.reshape(bsz, n_qc, q_block, N_HEADS, HEAD_DIM).transpose(1, 0, 2, 3, 4)
    pos = (q_pos0 + jnp.arange(lq)).reshape(n_qc, q_block)
    o = lax.map(attend_block, (qc, pos))
    return o.transpose(1, 0, 2, 3, 4).reshape(bsz, lq, N_HEADS * HEAD_DIM)


def moe_ffn(x, router_w, router_b, w_gate_up, b_gate_up, w_down, b_down):
    logits = (x @ router_w + router_b).astype(jnp.float32)
    top_val, top_idx = lax.top_k(logits, TOP_K)
    gates = jax.nn.softmax(top_val, axis=-1)
    combine = jnp.einsum('tk,tke->te', gates, jax.nn.one_hot(top_idx, N_EXPERTS, dtype=jnp.float32)).astype(x.dtype)
    out = jnp.zeros_like(x)
    for e in range(N_EXPERTS):
        h = x @ w_gate_up[e] + b_gate_up[e]
        gate, up = jnp.split(h, 2, axis=-1)
        gate = jnp.minimum(gate, SWIGLU_LIMIT)
        up = jnp.clip(up, -SWIGLU_LIMIT, SWIGLU_LIMIT)
        act = gate * jax.nn.sigmoid(SWIGLU_ALPHA * gate) * (up + 1.0)
        out = out + combine[:, e:e + 1] * (act @ w_down[e] + b_down[e])
    return out


def setup_inputs(seed: int = 0) -> dict:
    key = jax.random.key(seed)
    ks = jax.random.split(key, 26)
    f32 = jnp.float32

    def nrm(k, shape, scale):
        return jax.random.normal(k, shape, f32) * scale

    n_pages = PAST_LEN // PAGE_SIZE
    n_used = DEC_BATCH * n_pages
    n_phys = (n_used * 5) // 4
    page_table = jax.random.permutation(ks[0], n_phys)[:n_used].reshape(DEC_BATCH, n_pages).astype(jnp.int32)
    cache_shape = (N_B_LAYERS, n_phys, PAGE_SIZE, N_HEADS, HEAD_DIM)
    qkv_scale = jnp.concatenate([jnp.ones((2 * D_MODEL,), f32), jnp.full((D_MODEL,), DEEPNORM_BETA, f32)])
    return {
        'x_prompt': nrm(ks[1], (BATCH, SEQ, D_MODEL), 1.0),
        'x_sample': nrm(ks[2], (DEC_BATCH, DEC_SEQ, D_MODEL), 1.0),
        'cache_k': nrm(ks[3], cache_shape, 1.0),
        'cache_v': nrm(ks[4], cache_shape, 1.0),
        'page_table': page_table,
        'w_in_a': nrm(ks[5], (N_A_LAYERS, D_MODEL, 2 * D_A), D_MODEL ** -0.5),
        'b_in_a': nrm(ks[6], (N_A_LAYERS, 2 * D_A), 0.01),
        'ln_v_g': 1.0 + nrm(ks[7], (N_A_LAYERS, D_A), 0.01),
        'ln_v_b': nrm(ks[8], (N_A_LAYERS, D_A), 0.01),
        'w_s': nrm(ks[9], (N_A_LAYERS, N_GROUPS_A, CHUNK, CHUNK), 0.5 * CHUNK ** -0.5),
        'b_s': 1.0 + nrm(ks[10], (N_A_LAYERS, N_GROUPS_A, CHUNK), 0.01),
        'w_out_a': nrm(ks[11], (N_A_LAYERS, D_A, D_MODEL), D_A ** -0.5 * DEEPNORM_BETA),
        'w_qkv': nrm(ks[12], (N_B_LAYERS, D_MODEL, 3 * D_MODEL), D_MODEL ** -0.5) * qkv_scale,
        'w_o': nrm(ks[13], (N_B_LAYERS, D_MODEL, D_MODEL), D_MODEL ** -0.5 * DEEPNORM_BETA),
        'ln_mix_g': 1.0 + nrm(ks[14], (DEPTH, D_MODEL), 0.01),
        'ln_mix_b': nrm(ks[15], (DEPTH, D_MODEL), 0.01),
        'ln_ffn_g': 1.0 + nrm(ks[16], (DEPTH, D_MODEL), 0.01),
        'ln_ffn_b': nrm(ks[17], (DEPTH, D_MODEL), 0.01),
        'router_w': nrm(ks[18], (DEPTH, D_MODEL, N_EXPERTS), D_MODEL ** -0.5),
        'router_b': nrm(ks[19], (DEPTH, N_EXPERTS), 0.01),
        'w_gate_up': nrm(ks[20], (DEPTH, N_EXPERTS, D_MODEL, 2 * D_EXPERT), D_MODEL ** -0.5),
        'b_gate_up': nrm(ks[21], (DEPTH, N_EXPERTS, 2 * D_EXPERT), 0.01),
        'w_down': nrm(ks[22], (DEPTH, N_EXPERTS, D_EXPERT, D_MODEL), D_EXPERT ** -0.5 * DEEPNORM_BETA),
        'b_down': nrm(ks[23], (DEPTH, N_EXPERTS, D_MODEL), 0.01),
    }


def reference(x_prompt, x_sample, cache_k, cache_v, page_table, w_in_a, b_in_a, ln_v_g, ln_v_b, w_s, b_s,
              w_out_a, w_qkv, w_o, ln_mix_g, ln_mix_b, ln_ffn_g, ln_ffn_b, router_w, router_b,
              w_gate_up, b_gate_up, w_down, b_down):
    xp, xs = x_prompt, x_sample
    bp, sp = xp.shape[:2]
    bs, ss = xs.shape[:2]
    n_pages = page_table.shape[1]
    past_len = n_pages * PAGE_SIZE
    v_rows_p, v_rows_s, k_p, v_p, k_s, v_s = [], [], [], [], [], []
    for i in range(DEPTH):
        li = i // N_MIXERS
        if i % N_MIXERS == 0:
            mp, vrow_p = chunk_mlp(xp, w_in_a[li], b_in_a[li], ln_v_g[li], ln_v_b[li], w_s[li], b_s[li], w_out_a[li])
            ms, vrow_s = chunk_mlp(xs, w_in_a[li], b_in_a[li], ln_v_g[li], ln_v_b[li], w_s[li], b_s[li], w_out_a[li])
            last0 = ((sp - 1) // CHUNK) * CHUNK
            v_rows_p.append(vrow_p[:, last0:])
            v_rows_s.append(vrow_s)
        else:
            qp, kpn, vpn = qkv_proj(xp, w_qkv[li])
            mp = moba_attention(qp, kpn, vpn, 0, math.gcd(MOBA_QBLOCK, sp)) @ w_o[li]
            qs, ksn, vsn = qkv_proj(xs, w_qkv[li])
            past_k = cache_k[li, page_table].reshape(bs, past_len, N_HEADS, HEAD_DIM)
            past_v = cache_v[li, page_table].reshape(bs, past_len, N_HEADS, HEAD_DIM)
            k_all = jnp.concatenate([past_k, ksn], axis=1)
            v_all = jnp.concatenate([past_v, vsn], axis=1)
            ms = moba_attention(qs, k_all, v_all, past_len, ss) @ w_o[li]
            k_p.append(kpn)
            v_p.append(vpn)
            k_s.append(ksn)
            v_s.append(vsn)
        xp = layer_norm(DEEPNORM_ALPHA * xp + mp, ln_mix_g[i], ln_mix_b[i])
        xs = layer_norm(DEEPNORM_ALPHA * xs + ms, ln_mix_g[i], ln_mix_b[i])
        tokens = jnp.concatenate([xp.reshape(-1, D_MODEL), xs.reshape(-1, D_MODEL)], axis=0)
        f = moe_ffn(tokens, router_w[i], router_b[i], w_gate_up[i], b_gate_up[i], w_down[i], b_down[i])
        tokens = layer_norm(DEEPNORM_ALPHA * tokens + f, ln_ffn_g[i], ln_ffn_b[i])
        xp = tokens[:bp * sp].reshape(bp, sp, D_MODEL)
        xs = tokens[bp * sp:].reshape(bs, ss, D_MODEL)
    state_v_prompt = jnp.stack(v_rows_p)
    state_v_sample = jnp.stack(v_rows_s)
    k_new_prompt = jnp.stack(k_p)
    v_new_prompt = jnp.stack(v_p)
    k_new_sample = jnp.stack(k_s)
    v_new_sample = jnp.stack(v_s)
    return (xp, xs, state_v_prompt, state_v_sample, k_new_prompt, v_new_prompt, k_new_sample, v_new_sample)
```

```python
import functools

import jax
import jax.numpy as jnp
from jax import lax
from jax.experimental import pallas as pl
from jax.experimental.pallas import tpu as pltpu

CHUNK = 128
MOBA_BLOCK = 256
MOBA_TOPK = 3
TOP_K = 4
SWIGLU_LIMIT = 7.0
SWIGLU_ALPHA = 1.702
LN_EPS = 1e-5
NEG_BIG = -1e30
LANES = 128
ROW_TILE = 256
EXPERT_TILE = 512
VMEM_LIMIT = 56 * 1024 * 1024

F32 = jnp.float32
BF16 = jnp.bfloat16


def _round_up(n, m):
    return -(-n // m) * m


def _pick_tile(n, limit, mult):
    best = None
    for t in range(mult, min(n, limit) + 1, mult):
        if n % t == 0:
            best = t
    assert best is not None, (n, limit, mult)
    return best


def _params(sem):
    return pltpu.CompilerParams(dimension_semantics=sem, vmem_limit_bytes=VMEM_LIMIT)


def _layer_norm(y, g, b):
    mu = jnp.mean(y, axis=-1, keepdims=True)
    d = y - mu
    var = jnp.mean(d * d, axis=-1, keepdims=True)
    return d * lax.rsqrt(var + LN_EPS) * g + b


def _split_bf16(x):
    hi = x.astype(BF16)
    return hi, (x - hi.astype(F32)).astype(BF16)


def _dot3(x, w):
    xh, xl = _split_bf16(x)
    wh, wl = _split_bf16(w)
    m = x.shape[0]
    r = jnp.dot(jnp.concatenate([xh, xl], axis=0), wh, preferred_element_type=F32)
    return r[:m] + r[m:] + jnp.dot(xh, wl, preferred_element_type=F32)


def _mm_gelu_kernel(x_ref, w_ref, b_ref, o_ref, *scratch, hi, gelu):
    if hi:
        h = _dot3(x_ref[...], w_ref[...])
    else:
        xb_ref, = scratch

        @pl.when(pl.program_id(1) == 0)
        def _():
            xb_ref[...] = x_ref[...].astype(BF16)

        h = jnp.dot(xb_ref[...], w_ref[...].astype(BF16), preferred_element_type=F32)
    h = h + b_ref[...]
    o_ref[...] = 0.5 * h * (1.0 + lax.erf(h * (2.0 ** -0.5))) if gelu else h


def _mm_bias(x, w, b, *, gelu, hi=False):
    m, k = x.shape
    n = w.shape[1]
    tm = _pick_tile(m, 1024, 8)
    tn = _pick_tile(n, 512, LANES)
    return pl.pallas_call(
        functools.partial(_mm_gelu_kernel, hi=hi, gelu=gelu),
        grid=(m // tm, n // tn),
        in_specs=[pl.BlockSpec((tm, k), lambda i, j: (i, 0)),
                  pl.BlockSpec((k, tn), lambda i, j: (0, j)),
                  pl.BlockSpec((1, tn), lambda i, j: (0, j))],
        out_specs=pl.BlockSpec((tm, tn), lambda i, j: (i, j)),
        out_shape=jax.ShapeDtypeStruct((m, n), F32),
        scratch_shapes=[] if hi else [pltpu.VMEM((tm, k), BF16)],
        compiler_params=_params(("arbitrary", "arbitrary")),
        name="mm_bias_hi" if hi else "mm_bias",
    )(x, w, b.reshape(1, n))


def _mm_resid_ln_kernel(a_ref, w_ref, x_ref, g_ref, b_ref, o_ref, acc_ref, *, alpha, hi):
    kk = pl.program_id(1)
    if hi:
        part = _dot3(a_ref[...], w_ref[...])
    else:
        part = jnp.dot(a_ref[...].astype(BF16), w_ref[...].astype(BF16), preferred_element_type=F32)

    @pl.when(kk == 0)
    def _():
        acc_ref[...] = part

    @pl.when(kk > 0)
    def _():
        acc_ref[...] += part

    @pl.when(kk == pl.num_programs(1) - 1)
    def _():
        o_ref[...] = _layer_norm(alpha * x_ref[...] + acc_ref[...], g_ref[...], b_ref[...])


def _mm_resid_ln(a, w, x, g, b, alpha, hi=False):
    m, k = a.shape
    n = w.shape[1]
    tm = _pick_tile(m, 512, 8)
    tk = _pick_tile(k, 512, LANES)
    return pl.pallas_call(
        functools.partial(_mm_resid_ln_kernel, alpha=alpha, hi=hi),
        grid=(m // tm, k // tk),
        in_specs=[pl.BlockSpec((tm, tk), lambda i, j: (i, j)),
                  pl.BlockSpec((tk, n), lambda i, j: (j, 0)),
                  pl.BlockSpec((tm, n), lambda i, j: (i, 0)),
                  pl.BlockSpec((1, n), lambda i, j: (0, 0)),
                  pl.BlockSpec((1, n), lambda i, j: (0, 0))],
        out_specs=pl.BlockSpec((tm, n), lambda i, j: (i, 0)),
        out_shape=jax.ShapeDtypeStruct((m, n), F32),
        scratch_shapes=[pltpu.VMEM((tm, n), F32)],
        compiler_params=_params(("arbitrary", "arbitrary")),
        name="mm_resid_ln_hi" if hi else "mm_resid_ln",
    )(a, w, x, g.reshape(1, n), b.reshape(1, n))


def _spatial_mix_kernel(u_ref, v_ref, ws_ref, bs_ref, g_ref, b_ref, vn_ref, o_ref, *, n_groups):
    vn = _layer_norm(v_ref[...], g_ref[...], b_ref[...])
    vn_ref[...] = vn
    vb = vn.astype(BF16)
    gd = vn.shape[1] // n_groups
    for gi in range(n_groups):
        sl = slice(gi * gd, (gi + 1) * gd)
        mixed = jnp.dot(ws_ref[gi].astype(BF16), vb[:, sl], preferred_element_type=F32)
        mixed = mixed + bs_ref[:, gi:gi + 1]
        o_ref[:, sl] = (u_ref[:, sl] * mixed).astype(BF16)


def _spatial_mix(z, ws_causal, bs_t, ln_g, ln_b):
    t, two_da = z.shape
    da = two_da // 2
    n_groups = ws_causal.shape[0]
    return pl.pallas_call(
        functools.partial(_spatial_mix_kernel, n_groups=n_groups),
        grid=(t // CHUNK,),
        in_specs=[pl.BlockSpec((CHUNK, da), lambda i: (i, 0)),
                  pl.BlockSpec((CHUNK, da), lambda i: (i, 1)),
                  pl.BlockSpec((n_groups, CHUNK, CHUNK), lambda i: (0, 0, 0)),
                  pl.BlockSpec((CHUNK, n_groups), lambda i: (0, 0)),
                  pl.BlockSpec((1, da), lambda i: (0, 0)),
                  pl.BlockSpec((1, da), lambda i: (0, 0))],
        out_specs=[pl.BlockSpec((CHUNK, da), lambda i: (i, 0)),
                   pl.BlockSpec((CHUNK, da), lambda i: (i, 0))],
        out_shape=[jax.ShapeDtypeStruct((t, da), F32), jax.ShapeDtypeStruct((t, da), BF16)],
        compiler_params=_params(("arbitrary",)),
        name="spatial_mix",
    )(z, z, ws_causal, bs_t, ln_g.reshape(1, da), ln_b.reshape(1, da))


def _sample_gate_kernel(u_ref, v_ref, w_ref, c_ref, g_ref, b_ref, vn_ref, o_ref):
    vn = _layer_norm(v_ref[...], g_ref[...], b_ref[...])
    vn_ref[...] = vn
    o_ref[...] = u_ref[...] * (vn * w_ref[...] + c_ref[...])


def _sample_gate(z, w00, b0, ln_g, ln_b):
    m, two_da = z.shape
    da = two_da // 2
    row = pl.BlockSpec((1, da), lambda i: (0, 0))
    return pl.pallas_call(
        _sample_gate_kernel,
        grid=(1,),
        in_specs=[pl.BlockSpec((m, da), lambda i: (0, 0)), pl.BlockSpec((m, da), lambda i: (0, 1)),
                  row, row, row, row],
        out_specs=[pl.BlockSpec((m, da), lambda i: (0, 0)), pl.BlockSpec((m, da), lambda i: (0, 0))],
        out_shape=[jax.ShapeDtypeStruct((m, da), F32), jax.ShapeDtypeStruct((m, da), F32)],
        compiler_params=_params(("arbitrary",)),
        name="sample_gate",
    )(z, z, w00.reshape(1, da), b0.reshape(1, da), ln_g.reshape(1, da), ln_b.reshape(1, da))


def _block_ranks(g, n_valid, n_blocks):
    blk = lax.broadcasted_iota(jnp.int32, g.shape, 0)
    past = blk < n_valid
    rows = []
    for n in range(n_blocks):
        gn = g[n:n + 1, :]
        beats = jnp.where(past & ((g > gn) | ((g == gn) & (blk < n))), 1.0, 0.0)
        rows.append(jnp.sum(beats, axis=0, keepdims=True))
    return jnp.concatenate(rows, axis=0)


def _moba_prompt_kernel(slopes_ref, q_ref, k_ref, v_ref, o_ref,
                        kb_ref, vb_ref, kmean_ref, sel_ref, m_ref, l_ref, acc_ref, *, n_blocks, scale):
    h = pl.program_id(1)
    qi = pl.program_id(2)
    slope = slopes_ref[h]
    blk = MOBA_BLOCK
    dh = q_ref.shape[1]

    @pl.when(qi == 0)
    def _():
        k = k_ref[...]
        kb_ref[...] = k.astype(BF16)
        vb_ref[...] = v_ref[...].astype(BF16)
        kmean_ref[...] = jnp.zeros_like(kmean_ref)
        kmean_ref[0:n_blocks, :] = jnp.mean(k.reshape(n_blocks, blk, dh), axis=1)

    q = q_ref[...]
    gate_t = lax.dot_general(kmean_ref[...], q, (((1,), (1,)), ((), ())),
                             precision=lax.Precision.HIGHEST, preferred_element_type=F32)
    g = gate_t[0:n_blocks, :]
    rank = _block_ranks(g, qi, n_blocks)
    bidx = lax.broadcasted_iota(jnp.int32, g.shape, 0)
    sel_t = jnp.where((rank < MOBA_TOPK) & (bidx < qi), 1.0, 0.0)
    sel_t = jnp.concatenate([sel_t, jnp.zeros((LANES - n_blocks, blk), F32)], axis=0)
    sel = sel_t.T
    for n in range(n_blocks):
        sel_ref[n] = jnp.broadcast_to(sel[:, n:n + 1], (blk, blk))

    qb = q.astype(BF16)
    row = lax.broadcasted_iota(jnp.int32, (blk, blk), 0)
    col = lax.broadcasted_iota(jnp.int32, (blk, blk), 1)
    dist0 = (row - col).astype(F32)

    def scores(j):
        kj = kb_ref[pl.ds(pl.multiple_of(j * blk, blk), blk), :]
        return lax.dot_general(qb, kj, (((1,), (1,)), ((), ())), preferred_element_type=F32) * scale

    def pv(p, j):
        vj = vb_ref[pl.ds(pl.multiple_of(j * blk, blk), blk), :]
        return jnp.dot(p.astype(BF16), vj, preferred_element_type=F32)

    s = scores(qi) - slope * jnp.abs(dist0)
    s = jnp.where(col <= row, s, NEG_BIG)
    m0 = jnp.max(s, axis=-1, keepdims=True)
    p = jnp.exp(s - m0)
    m_ref[...] = m0
    l_ref[...] = jnp.sum(p, axis=-1, keepdims=True)
    acc_ref[...] = pv(p, qi)

    def body(j, carry):
        off = ((qi - j) * blk).astype(F32)
        s = scores(j) - slope * (dist0 + off)
        s = jnp.where(sel_ref[j] > 0.5, s, NEG_BIG)
        m_old = m_ref[...]
        m_new = jnp.maximum(m_old, jnp.max(s, axis=-1, keepdims=True))
        a = jnp.exp(m_old - m_new)
        p = jnp.exp(s - m_new)
        l_ref[...] = a * l_ref[...] + jnp.sum(p, axis=-1, keepdims=True)
        acc_ref[...] = a * acc_ref[...] + pv(p, j)
        m_ref[...] = m_new
        return carry

    lax.fori_loop(0, qi, body, 0)
    o_ref[...] = acc_ref[...] / l_ref[...]


def _moba_prompt(qkv, slopes, bp, sp, n_heads):
    d = qkv.shape[1] // 3
    dh = d // n_heads
    assert sp % MOBA_BLOCK == 0 and dh % LANES == 0
    nq = sp // MOBA_BLOCK
    assert nq <= LANES
    hb = dh // LANES
    del hb
    return pl.pallas_call(
        functools.partial(_moba_prompt_kernel, n_blocks=nq, scale=dh ** -0.5),
        grid=(bp, n_heads, nq),
        in_specs=[pl.BlockSpec(memory_space=pltpu.SMEM),
                  pl.BlockSpec((MOBA_BLOCK, dh), lambda b, h, i: (b * nq + i, h)),
                  pl.BlockSpec((sp, dh), lambda b, h, i: (b, n_heads + h)),
                  pl.BlockSpec((sp, dh), lambda b, h, i: (b, 2 * n_heads + h))],
        out_specs=pl.BlockSpec((MOBA_BLOCK, dh), lambda b, h, i: (b * nq + i, h)),
        out_shape=jax.ShapeDtypeStruct((bp * sp, d), F32),
        scratch_shapes=[pltpu.VMEM((sp, dh), BF16),
                        pltpu.VMEM((sp, dh), BF16),
                        pltpu.VMEM((LANES, dh), F32),
                        pltpu.VMEM((nq, MOBA_BLOCK, MOBA_BLOCK), F32),
                        pltpu.VMEM((MOBA_BLOCK, 1), F32),
                        pltpu.VMEM((MOBA_BLOCK, 1), F32),
                        pltpu.VMEM((MOBA_BLOCK, dh), F32)],
        compiler_params=_params(("arbitrary", "arbitrary", "arbitrary")),
        name="moba_prompt",
    )(slopes, qkv, qkv, qkv)


def _page_sum_kernel(pt_ref, page_ref, o_ref, *, pages_per_block):
    del pt_ref
    p = pl.program_id(1)

    @pl.when(p == 0)
    def _():
        o_ref[...] = jnp.zeros_like(o_ref)

    s = jnp.sum(page_ref[...], axis=0, keepdims=True)
    r = p // pages_per_block
    o_ref[pl.ds(r, 1), :] += s


def _past_block_sums(cache4, layer, page_table):
    _, _, page, hd = cache4.shape
    bs, n_pages = page_table.shape
    ppb = MOBA_BLOCK // page
    nblk = n_pages // ppb
    grid_spec = pltpu.PrefetchScalarGridSpec(
        num_scalar_prefetch=1,
        grid=(bs, n_pages),
        in_specs=[pl.BlockSpec((None, None, page, hd),
                               lambda b, p, pt: (layer, pt[b * n_pages + p], 0, 0))],
        out_specs=pl.BlockSpec((None, nblk, hd), lambda b, p, pt: (b, 0, 0)),
    )
    return pl.pallas_call(
        functools.partial(_page_sum_kernel, pages_per_block=ppb),
        grid_spec=grid_spec,
        out_shape=jax.ShapeDtypeStruct((bs, nblk, hd), F32),
        compiler_params=_params(("arbitrary", "arbitrary")),
        name="page_sum",
    )(page_table.reshape(-1), cache4)


def _sample_select_kernel(q_ref, ksum_ref, seg_ref, o_ref, *, n_blocks):
    prod = ksum_ref[...] * (1.0 / MOBA_BLOCK) * q_ref[...]
    gate = jnp.dot(prod, seg_ref[...], precision=lax.Precision.HIGHEST, preferred_element_type=F32)
    rank = _block_ranks(gate, n_blocks, n_blocks)
    bidx = lax.broadcasted_iota(jnp.int32, gate.shape, 0).astype(F32)
    rows = []
    for r in range(MOBA_TOPK):
        rows.append(jnp.sum(jnp.where(rank == float(r), bidx, 0.0), axis=0, keepdims=True))
    rows.append(jnp.zeros((8 - MOBA_TOPK, gate.shape[1]), F32))
    o_ref[...] = jnp.concatenate(rows, axis=0).astype(jnp.int32)


def _sample_select(q_s, ksum, n_heads):
    bs, nblk, hd = ksum.shape
    dh = hd // n_heads
    assert nblk >= MOBA_TOPK and nblk % 8 == 0 and n_heads <= LANES
    seg = (jnp.arange(hd)[:, None] // dh == jnp.arange(LANES)[None, :]).astype(F32)
    out = pl.pallas_call(
        functools.partial(_sample_select_kernel, n_blocks=nblk),
        grid=(bs,),
        in_specs=[pl.BlockSpec((None, 1, hd), lambda b: (b, 0, 0)),
                  pl.BlockSpec((None, nblk, hd), lambda b: (b, 0, 0)),
                  pl.BlockSpec((hd, LANES), lambda b: (0, 0))],
        out_specs=pl.BlockSpec((None, 8, LANES), lambda b: (b, 0, 0)),
        out_shape=jax.ShapeDtypeStruct((bs, 8, LANES), jnp.int32),
        compiler_params=_params(("arbitrary",)),
        name="sample_select",
    )(q_s.reshape(bs, 1, hd), ksum, seg)
    return out[:, :MOBA_TOPK, :n_heads].transpose(0, 2, 1)


def _sample_attend_kernel(pt_ref, sel_ref, slopes_ref, q_ref, kn_ref, vn_ref, k_ref, v_ref, o_ref,
                          m_ref, l_ref, acc_ref, *, n_heads, pages_per_block, past_len, scale):
    del pt_ref
    b = pl.program_id(0)
    h = pl.program_id(1)
    s_id = pl.program_id(2)
    page = k_ref.shape[0]
    dh = k_ref.shape[1]
    slope = slopes_ref[h]
    hi = lax.Precision.HIGHEST
    q = jnp.broadcast_to(q_ref[pl.ds(h, 1), :], (8, dh))

    @pl.when(s_id == 0)
    def _():
        s_new = jnp.sum(q[0:1, :] * kn_ref[pl.ds(h, 1), :], axis=-1, keepdims=True) * scale
        m_ref[...] = jnp.broadcast_to(s_new, m_ref.shape)
        l_ref[...] = jnp.ones_like(l_ref)
        acc_ref[...] = vn_ref[pl.ds(h, 1), :]

    blk_idx = sel_ref[(b * n_heads + h) * MOBA_TOPK + s_id // pages_per_block]
    k_pos0 = blk_idx * MOBA_BLOCK + (s_id % pages_per_block) * page
    lane = lax.broadcasted_iota(jnp.int32, (1, page), 1)
    dist = (past_len - k_pos0 - lane).astype(F32)
    s = lax.dot_general(q, k_ref[...], (((1,), (1,)), ((), ())), precision=hi,
                        preferred_element_type=F32)[0:1, :] * scale
    s = s - slope * dist
    m_old = m_ref[:, 0:1]
    m_new = jnp.maximum(m_old, jnp.max(s, axis=-1, keepdims=True))
    a = jnp.exp(m_old - m_new)
    p = jnp.exp(s - m_new)
    l_new = a * l_ref[:, 0:1] + jnp.sum(p, axis=-1, keepdims=True)
    l_ref[...] = jnp.broadcast_to(l_new, l_ref.shape)
    pv = jnp.dot(jnp.broadcast_to(p, (8, page)), v_ref[...], precision=hi, preferred_element_type=F32)
    acc_ref[...] = a * acc_ref[...] + pv[0:1, :]
    m_ref[...] = jnp.broadcast_to(m_new, m_ref.shape)

    @pl.when(s_id == pl.num_programs(2) - 1)
    def _():
        o_ref[pl.ds(h, 1), :] = acc_ref[...] / l_ref[:, 0:1]


def _sample_attend(q_s, k_new, v_new, cache_k4, cache_v4, layer, page_table, sel, slopes, n_heads):
    bs, hd = q_s.shape
    dh = hd // n_heads
    page = cache_k4.shape[2]
    n_pages = page_table.shape[1]
    ppb = MOBA_BLOCK // page
    steps = MOBA_TOPK * ppb

    def kv_map(b, h, s, pt, sl):
        blk = sl[(b * n_heads + h) * MOBA_TOPK + s // ppb]
        return (layer, pt[b * n_pages + blk * ppb + s % ppb], 0, h)

    row_spec = pl.BlockSpec((None, n_heads, dh), lambda b, h, s, pt, sl: (b, 0, 0))
    grid_spec = pltpu.PrefetchScalarGridSpec(
        num_scalar_prefetch=2,
        grid=(bs, n_heads, steps),
        in_specs=[pl.BlockSpec(memory_space=pltpu.SMEM),
                  row_spec, row_spec, row_spec,
                  pl.BlockSpec((None, None, page, dh), kv_map),
                  pl.BlockSpec((None, None, page, dh), kv_map)],
        out_specs=pl.BlockSpec((None, n_heads, dh), lambda b, h, s, pt, sl: (b, 0, 0)),
        scratch_shapes=[pltpu.VMEM((1, LANES), F32), pltpu.VMEM((1, LANES), F32), pltpu.VMEM((1, dh), F32)],
    )
    out = pl.pallas_call(
        functools.partial(_sample_attend_kernel, n_heads=n_heads, pages_per_block=ppb,
                          past_len=n_pages * page, scale=dh ** -0.5),
        grid_spec=grid_spec,
        out_shape=jax.ShapeDtypeStruct((bs, n_heads, dh), F32),
        compiler_params=_params(("arbitrary", "arbitrary", "arbitrary")),
        name="sample_attend",
    )(page_table.reshape(-1), sel.reshape(-1), slopes,
      q_s.reshape(bs, n_heads, dh), k_new.reshape(bs, n_heads, dh), v_new.reshape(bs, n_heads, dh),
      cache_k4, cache_v4)
    return out.reshape(bs, hd)


def _router_kernel(x_ref, w_ref, b_ref, slab_ref, cnt_ref, carry_ref, *, n_tokens):
    i = pl.program_id(0)
    tm = x_ref.shape[0]

    @pl.when(i == 0)
    def _():
        carry_ref[...] = jnp.zeros_like(carry_ref)

    logits = jnp.dot(x_ref[...], w_ref[...], precision=lax.Precision.HIGHEST,
                     preferred_element_type=F32) + b_ref[...]
    lane = lax.broadcasted_iota(jnp.int32, logits.shape, 1)
    lane_f = lane.astype(F32)
    valid = (lax.broadcasted_iota(jnp.int32, (tm, 1), 0) + i * tm) < n_tokens
    vals, onehots = [], []
    work = logits
    for _ in range(TOP_K):
        mx = jnp.max(work, axis=-1, keepdims=True)
        idx = jnp.min(jnp.where(work == mx, lane_f, float(LANES)), axis=-1, keepdims=True)
        oh = lane_f == idx
        vals.append(mx)
        onehots.append(jnp.where(oh & valid, 1.0, 0.0))
        work = jnp.where(oh, -3e38, work)
    exps = [jnp.exp(v - vals[0]) for v in vals]
    denom = exps[0]
    for e in exps[1:]:
        denom = denom + e
    oh_all = onehots[0]
    for oh in onehots[1:]:
        oh_all = oh_all + oh
    r = lax.broadcasted_iota(jnp.int32, (tm, tm), 0)
    c = lax.broadcasted_iota(jnp.int32, (tm, tm), 1)
    tri = jnp.where(c < r, 1.0, 0.0).astype(BF16)
    before = jnp.dot(tri, oh_all.astype(BF16), preferred_element_type=F32) + carry_ref[...]
    slab = jnp.zeros(logits.shape, F32)
    for kk in range(TOP_K):
        idx_k = jnp.sum(onehots[kk] * lane_f, axis=-1, keepdims=True)
        rank_k = jnp.sum(onehots[kk] * before, axis=-1, keepdims=True)
        gate_k = jnp.where(valid, exps[kk] / denom, 0.0)
        slab = jnp.where(lane == kk, idx_k, slab)
        slab = jnp.where(lane == TOP_K + kk, gate_k, slab)
        slab = jnp.where(lane == 2 * TOP_K + kk, rank_k, slab)
    slab_ref[...] = slab
    carry_ref[...] += jnp.sum(oh_all, axis=0, keepdims=True)
    cnt_ref[...] = jnp.broadcast_to(carry_ref[...], cnt_ref.shape)


def _router(x, w, b, n_tokens):
    t, d = x.shape
    n_exp = w.shape[1]
    assert TOP_K <= n_exp <= LANES and 3 * TOP_K <= LANES
    w_pad = jnp.zeros((d, LANES), F32).at[:, :n_exp].set(w)
    b_pad = jnp.full((1, LANES), NEG_BIG, F32).at[0, :n_exp].set(b)
    return pl.pallas_call(
        functools.partial(_router_kernel, n_tokens=n_tokens),
        grid=(t // ROW_TILE,),
        in_specs=[pl.BlockSpec((ROW_TILE, d), lambda i: (i, 0)),
                  pl.BlockSpec((d, LANES), lambda i: (0, 0)),
                  pl.BlockSpec((1, LANES), lambda i: (0, 0))],
        out_specs=[pl.BlockSpec((ROW_TILE, LANES), lambda i: (i, 0)),
                   pl.BlockSpec((8, LANES), lambda i: (0, 0))],
        out_shape=[jax.ShapeDtypeStruct((t, LANES), F32), jax.ShapeDtypeStruct((8, LANES), F32)],
        scratch_shapes=[pltpu.VMEM((1, LANES), F32)],
        compiler_params=_params(("arbitrary",)),
        name="router",
    )(x, w_pad, b_pad)


def _gather_rows_kernel(idx_ref, nact_ref, x_hbm, o_ref, sem):
    i = pl.program_id(0)
    rows = o_ref.shape[0]

    def row_copy(r, src_row):
        return pltpu.make_async_copy(x_hbm.at[pl.ds(src_row, 1), :], o_ref.at[pl.ds(r, 1), :], sem)

    @pl.when(i < nact_ref[0])
    def _():
        def start(r, carry):
            row_copy(r, idx_ref[i * rows + r]).start()
            return carry

        lax.fori_loop(0, rows, start, 0)

        def wait(r, carry):
            row_copy(r, 0).wait()
            return carry

        lax.fori_loop(0, rows, wait, 0)

    @pl.when(i >= nact_ref[0])
    def _():
        o_ref[...] = jnp.zeros_like(o_ref)


def _gather_rows(x, idx, n_active_tiles, rows_per_tile):
    n = idx.shape[0]
    d = x.shape[1]
    grid_spec = pltpu.PrefetchScalarGridSpec(
        num_scalar_prefetch=2,
        grid=(n // rows_per_tile,),
        in_specs=[pl.BlockSpec(memory_space=pl.ANY)],
        out_specs=pl.BlockSpec((rows_per_tile, d), lambda i, ix, na: (i, 0)),
        scratch_shapes=[pltpu.SemaphoreType.DMA(())],
    )
    return pl.pallas_call(
        _gather_rows_kernel,
        grid_spec=grid_spec,
        out_shape=jax.ShapeDtypeStruct((n, d), x.dtype),
        compiler_params=_params(("arbitrary",)),
        name="gather_rows",
    )(idx, n_active_tiles, x)


def _expert_kernel(te_ref, nact_ref, x_ref, wg_ref, wu_ref, wd_ref, bg_ref, bu_ref, bd_ref, o_ref, xb_ref):
    del te_ref
    i = pl.program_id(0)
    c = pl.program_id(1)
    active = i < nact_ref[0]

    @pl.when(active & (c == 0))
    def _():
        xb_ref[...] = x_ref[...].astype(BF16)

    @pl.when(active)
    def _():
        xb = xb_ref[...]
        gate = jnp.dot(xb, wg_ref[...].astype(BF16), preferred_element_type=F32) + bg_ref[...]
        up = jnp.dot(xb, wu_ref[...].astype(BF16), preferred_element_type=F32) + bu_ref[...]
        gate = jnp.minimum(gate, SWIGLU_LIMIT)
        up = jnp.clip(up, -SWIGLU_LIMIT, SWIGLU_LIMIT)
        act = gate * (1.0 / (1.0 + jnp.exp(-SWIGLU_ALPHA * gate))) * (up + 1.0)
        part = jnp.dot(act.astype(BF16), wd_ref[...].astype(BF16), preferred_element_type=F32)

        @pl.when(c == 0)
        def _():
            o_ref[...] = part + bd_ref[...]

        @pl.when(c > 0)
        def _():
            o_ref[...] += part

    @pl.when(jnp.logical_not(active) & (c == 0))
    def _():
        o_ref[...] = jnp.zeros_like(o_ref)


def _experts(xs, tile_expert, n_active, w_gate_up, b_gate_up, w_down, b_down, layer):
    p, d = xs.shape
    n_exp, _, two_de = w_gate_up.shape[1:]
    de = two_de // 2
    tn = _pick_tile(de, 256, LANES)
    nc = de // tn
    bgu = b_gate_up.reshape(b_gate_up.shape[0], n_exp, 1, two_de)
    bdn = b_down.reshape(b_down.shape[0], n_exp, 1, d)
    grid_spec = pltpu.PrefetchScalarGridSpec(
        num_scalar_prefetch=2,
        grid=(p // EXPERT_TILE, nc),
        in_specs=[pl.BlockSpec((EXPERT_TILE, d), lambda i, c, te, na: (i, 0)),
                  pl.BlockSpec((None, None, d, tn), lambda i, c, te, na: (layer, te[i], 0, c)),
                  pl.BlockSpec((None, None, d, tn), lambda i, c, te, na: (layer, te[i], 0, nc + c)),
                  pl.BlockSpec((None, None, tn, d), lambda i, c, te, na: (layer, te[i], c, 0)),
                  pl.BlockSpec((None, None, 1, tn), lambda i, c, te, na: (layer, te[i], 0, c)),
                  pl.BlockSpec((None, None, 1, tn), lambda i, c, te, na: (layer, te[i], 0, nc + c)),
                  pl.BlockSpec((None, None, 1, d), lambda i, c, te, na: (layer, te[i], 0, 0))],
        out_specs=pl.BlockSpec((EXPERT_TILE, d), lambda i, c, te, na: (i, 0)),
        scratch_shapes=[pltpu.VMEM((EXPERT_TILE, d), BF16)],
    )
    return pl.pallas_call(
        _expert_kernel,
        grid_spec=grid_spec,
        out_shape=jax.ShapeDtypeStruct((p, d), F32),
        compiler_params=_params(("arbitrary", "arbitrary")),
        name="experts",
    )(tile_expert, n_active, xs, w_gate_up, w_gate_up, w_down, bgu, bgu, bdn)


def _sample_expert_kernel(el_ref, nu_ref, x_ref, comb_ref, wg_ref, wu_ref, wd_ref, bg_ref, bu_ref, bd_ref,
                          g_ref, b_ref, o_ref, acc_ref, *, alpha):
    del el_ref
    j = pl.program_id(0)
    c = pl.program_id(1)

    @pl.when((j == 0) & (c == 0))
    def _():
        acc_ref[...] = jnp.zeros_like(acc_ref)

    @pl.when(j < nu_ref[0])
    def _():
        x = x_ref[...]
        gate = jnp.minimum(_dot3(x, wg_ref[...]) + bg_ref[...], SWIGLU_LIMIT)
        up = jnp.clip(_dot3(x, wu_ref[...]) + bu_ref[...], -SWIGLU_LIMIT, SWIGLU_LIMIT)
        act = gate * (1.0 / (1.0 + jnp.exp(-SWIGLU_ALPHA * gate))) * (up + 1.0)
        part = _dot3(act, wd_ref[...])
        part = part + jnp.where(c == 0, 1.0, 0.0) * bd_ref[...]
        acc_ref[...] += comb_ref[:, 0:1] * part

    @pl.when((j == pl.num_programs(0) - 1) & (c == pl.num_programs(1) - 1))
    def _():
        o_ref[...] = _layer_norm(alpha * x_ref[...] + acc_ref[...], g_ref[...], b_ref[...])


def _sample_moe_layer(x, layer, router_w, router_b, w_gate_up, b_gate_up, w_down, b_down, ln_g, ln_b, alpha):
    m, d = x.shape
    n_exp = router_w.shape[1]
    two_de = w_gate_up.shape[3]
    de = two_de // 2
    x_pad = jnp.zeros((ROW_TILE, d), F32).at[:m].set(x)
    slab, _ = _router(x_pad, router_w, router_b, m)
    idx = slab[:m, :TOP_K].astype(jnp.int32)
    gates = slab[:m, TOP_K:2 * TOP_K]
    comb = jnp.einsum("tk,tke->te", gates, jax.nn.one_hot(idx, n_exp, dtype=F32))
    used = jnp.any(comb > 0.0, axis=0)
    n_used = jnp.sum(used).astype(jnp.int32)
    order = jnp.argsort(jnp.logical_not(used), stable=True).astype(jnp.int32)
    n_slots = min(n_exp, m * TOP_K)
    slots = jnp.minimum(jnp.arange(n_slots, dtype=jnp.int32), n_used - 1)
    e_list = order[slots]
    comb_b = jnp.broadcast_to(comb.T[:, :, None], (n_exp, m, LANES))
    tn = _pick_tile(de, 256, LANES)
    nc = de // tn
    bgu = b_gate_up.reshape(b_gate_up.shape[0], n_exp, 1, two_de)
    bdn = b_down.reshape(b_down.shape[0], n_exp, 1, d)

    def chunk(j, c, nu):
        return jnp.where(j < nu[0], c, nc - 1)

    row = pl.BlockSpec((1, d), lambda j, c, el, nu: (0, 0))
    grid_spec = pltpu.PrefetchScalarGridSpec(
        num_scalar_prefetch=2,
        grid=(n_slots, nc),
        in_specs=[pl.BlockSpec((m, d), lambda j, c, el, nu: (0, 0)),
                  pl.BlockSpec((None, m, LANES), lambda j, c, el, nu: (el[j], 0, 0)),
                  pl.BlockSpec((None, None, d, tn), lambda j, c, el, nu: (layer, el[j], 0, chunk(j, c, nu))),
                  pl.BlockSpec((None, None, d, tn), lambda j, c, el, nu: (layer, el[j], 0, nc + chunk(j, c, nu))),
                  pl.BlockSpec((None, None, tn, d), lambda j, c, el, nu: (layer, el[j], chunk(j, c, nu), 0)),
                  pl.BlockSpec((None, None, 1, tn), lambda j, c, el, nu: (layer, el[j], 0, chunk(j, c, nu))),
                  pl.BlockSpec((None, None, 1, tn), lambda j, c, el, nu: (layer, el[j], 0, nc + chunk(j, c, nu))),
                  pl.BlockSpec((None, None, 1, d), lambda j, c, el, nu: (layer, el[j], 0, 0)),
                  row, row],
        out_specs=pl.BlockSpec((m, d), lambda j, c, el, nu: (0, 0)),
        scratch_shapes=[pltpu.VMEM((m, d), F32)],
    )
    return pl.pallas_call(
        functools.partial(_sample_expert_kernel, alpha=alpha),
        grid_spec=grid_spec,
        out_shape=jax.ShapeDtypeStruct((m, d), F32),
        compiler_params=_params(("arbitrary", "arbitrary")),
        name="sample_experts",
    )(e_list, n_used.reshape(1), x, comb_b, w_gate_up, w_gate_up, w_down, bgu, bgu, bdn,
      ln_g.reshape(1, d), ln_b.reshape(1, d))


def _combine_ln_kernel(y_ref, slab_ref, x_ref, g_ref, b_ref, o_ref, *, alpha):
    d = x_ref.shape[1]
    f = jnp.zeros(x_ref.shape, F32)
    for kk in range(TOP_K):
        f = f + slab_ref[:, TOP_K + kk:TOP_K + kk + 1] * y_ref[:, kk * d:(kk + 1) * d]
    o_ref[...] = _layer_norm(alpha * x_ref[...] + f, g_ref[...], b_ref[...])


def _combine_ln(y4, slab, x, g, b, alpha):
    t, d = x.shape
    return pl.pallas_call(
        functools.partial(_combine_ln_kernel, alpha=alpha),
        grid=(t // ROW_TILE,),
        in_specs=[pl.BlockSpec((ROW_TILE, TOP_K * d), lambda i: (i, 0)),
                  pl.BlockSpec((ROW_TILE, LANES), lambda i: (i, 0)),
                  pl.BlockSpec((ROW_TILE, d), lambda i: (i, 0)),
                  pl.BlockSpec((1, d), lambda i: (0, 0)),
                  pl.BlockSpec((1, d), lambda i: (0, 0))],
        out_specs=pl.BlockSpec((ROW_TILE, d), lambda i: (i, 0)),
        out_shape=jax.ShapeDtypeStruct((t, d), F32),
        compiler_params=_params(("arbitrary",)),
        name="combine_ln",
    )(y4, slab, x, g.reshape(1, d), b.reshape(1, d))


def _moe_layer(x, n_tokens, layer, router_w, router_b, w_gate_up, b_gate_up, w_down, b_down, ln_g, ln_b, alpha):
    t, d = x.shape
    n_exp = router_w.shape[1]
    slab, cnt = _router(x, router_w, router_b, n_tokens)
    idx = slab[:, :TOP_K].astype(jnp.int32)
    rank = slab[:, 2 * TOP_K:3 * TOP_K].astype(jnp.int32)
    counts = cnt[0, :n_exp].astype(jnp.int32)
    n_tiles = (n_tokens * TOP_K + n_exp * (EXPERT_TILE - 1)) // EXPERT_TILE
    p_rows = n_tiles * EXPERT_TILE
    tiles_per_exp = (counts + EXPERT_TILE - 1) // EXPERT_TILE
    tile_end = jnp.cumsum(tiles_per_exp)
    n_active = tile_end[-1]
    row_off = (tile_end - tiles_per_exp) * EXPERT_TILE
    valid = (jnp.arange(t) < n_tokens)[:, None]
    pos = jnp.where(valid, row_off[idx] + rank, 0)
    tok = jnp.broadcast_to(jnp.arange(t, dtype=jnp.int32)[:, None], pos.shape)
    src = jnp.zeros((p_rows,), jnp.int32).at[jnp.where(valid, pos, p_rows).reshape(-1)].set(
        tok.reshape(-1), mode="drop")
    tile_ids = jnp.minimum(jnp.arange(n_tiles, dtype=jnp.int32), n_active - 1)
    tile_expert = jnp.minimum(jnp.searchsorted(tile_end, tile_ids, side="right"), n_exp - 1).astype(jnp.int32)
    n_active = n_active.reshape(1).astype(jnp.int32)

    xs = _gather_rows(x, src, n_active, EXPERT_TILE)
    ys = _experts(xs, tile_expert, n_active, w_gate_up, b_gate_up, w_down, b_down, layer)
    all_tiles = jnp.full((1,), (t * TOP_K) // EXPERT_TILE, jnp.int32)
    y4 = _gather_rows(ys, pos.reshape(-1), all_tiles, EXPERT_TILE)
    return _combine_ln(y4.reshape(t, TOP_K * d), slab, x, ln_g, ln_b, alpha)


def _gmlp_prompt(x, w_in, b_in, lnv_g, lnv_b, w_s, b_s, w_out, ln_g, ln_b, alpha):
    z = _mm_bias(x, w_in, b_in, gelu=True)
    causal = jnp.tril(jnp.ones((CHUNK, CHUNK), dtype=bool))
    vn, gated = _spatial_mix(z, jnp.where(causal[None], w_s, 0.0), b_s.T, lnv_g, lnv_b)
    return _mm_resid_ln(gated, w_out, x, ln_g, ln_b, alpha), vn


def _gmlp_sample(x, w_in, b_in, lnv_g, lnv_b, w_s, b_s, w_out, ln_g, ln_b, alpha):
    z = _mm_bias(x, w_in, b_in, gelu=True, hi=True)
    group_dim = lnv_g.shape[0] // w_s.shape[0]
    vn, gated = _sample_gate(z, jnp.repeat(w_s[:, 0, 0], group_dim), jnp.repeat(b_s[:, 0], group_dim), lnv_g, lnv_b)
    return _mm_resid_ln(gated, w_out, x, ln_g, ln_b, alpha, hi=True), vn


def kernel(x_prompt, x_sample, cache_k, cache_v, page_table, w_in_a, b_in_a, ln_v_g, ln_v_b, w_s, b_s, w_out_a, w_qkv, w_o, ln_mix_g, ln_mix_b, ln_ffn_g, ln_ffn_b, router_w, router_b, w_gate_up, b_gate_up, w_down, b_down):
    bp, sp, d = x_prompt.shape
    bs, ss, _ = x_sample.shape
    depth = ln_mix_g.shape[0]
    n_heads = cache_k.shape[3]
    dh = cache_k.shape[4]
    page = cache_k.shape[2]
    n_pages = page_table.shape[1]
    assert ss == 1 and sp % CHUNK == 0 and sp % MOBA_BLOCK == 0
    assert MOBA_BLOCK % page == 0 and (n_pages * page) % MOBA_BLOCK == 0
    alpha = (2 * depth) ** 0.25
    n_prompt = bp * sp
    n_tokens = n_prompt + bs
    slopes = 2.0 ** (-8.0 * jnp.arange(1, n_heads + 1, dtype=F32) / n_heads)
    cache_k4 = cache_k.reshape(cache_k.shape[0], cache_k.shape[1], page, n_heads * dh)
    cache_v4 = cache_v.reshape(cache_v.shape[0], cache_v.shape[1], page, n_heads * dh)
    no_bias = jnp.zeros((3 * d,), F32)

    xp = x_prompt.reshape(n_prompt, d)
    xs = x_sample.reshape(bs, d)
    v_rows_p, v_rows_s, k_p, v_p, k_s, v_s = [], [], [], [], [], []
    for i in range(depth):
        li = i // 2
        g_mix, b_mix = ln_mix_g[i], ln_mix_b[i]
        if i % 2 == 0:
            mixer = (w_in_a[li], b_in_a[li], ln_v_g[li], ln_v_b[li], w_s[li], b_s[li], w_out_a[li], g_mix, b_mix, alpha)
            xp, vn_p = _gmlp_prompt(xp, *mixer)
            xs, vn_s = _gmlp_sample(xs, *mixer)
            last0 = ((sp - 1) // CHUNK) * CHUNK
            v_rows_p.append(vn_p.reshape(bp, sp, -1)[:, last0:])
            v_rows_s.append(vn_s.reshape(bs, ss, -1))
        else:
            qkv = _mm_bias(xp, w_qkv[li], no_bias, gelu=False)
            attn_p = _moba_prompt(qkv, slopes, bp, sp, n_heads)
            xp = _mm_resid_ln(attn_p, w_o[li], xp, g_mix, b_mix, alpha)
            k_p.append(qkv[:, d:2 * d].reshape(bp, sp, n_heads, dh))
            v_p.append(qkv[:, 2 * d:].reshape(bp, sp, n_heads, dh))

            qkv_s = _mm_bias(xs, w_qkv[li], no_bias, gelu=False, hi=True)
            q_s, k_new, v_new = qkv_s[:, :d], qkv_s[:, d:2 * d], qkv_s[:, 2 * d:]
            ksum = _past_block_sums(cache_k4, li, page_table)
            sel = _sample_select(q_s, ksum, n_heads)
            attn_s = _sample_attend(q_s, k_new, v_new, cache_k4, cache_v4, li, page_table, sel, slopes, n_heads)
            xs = _mm_resid_ln(attn_s, w_o[li], xs, g_mix, b_mix, alpha, hi=True)
            k_s.append(k_new.reshape(bs, ss, n_heads, dh))
            v_s.append(v_new.reshape(bs, ss, n_heads, dh))
        moe = (router_w[i], router_b[i], w_gate_up, b_gate_up, w_down, b_down, ln_ffn_g[i], ln_ffn_b[i], alpha)
        if i < depth - 1:
            xp = _moe_layer(xp, n_prompt, i, *moe)
            xs = _sample_moe_layer(xs, i, *moe)
        else:
            t = _round_up(n_tokens, ROW_TILE)
            x = _moe_layer(jnp.concatenate([xp, xs, jnp.zeros((t - n_tokens, d), F32)], axis=0), n_tokens, i, *moe)
            xp, xs = x[:n_prompt], x[n_prompt:n_tokens]
    return (xp.reshape(bp, sp, d), xs.reshape(bs, ss, d), jnp.stack(v_rows_p), jnp.stack(v_rows_s),
            jnp.stack(k_p), jnp.stack(v_p), jnp.stack(k_s), jnp.stack(v_s))
```

```python
import functools

import jax
import jax.numpy as jnp
from jax import lax
from jax.experimental import pallas as pl
from jax.experimental.pallas import tpu as pltpu

CHUNK = 128
MOBA_BLOCK = 256
MOBA_TOPK = 3
TOP_K = 4
SWIGLU_LIMIT = 7.0
SWIGLU_ALPHA = 1.702
LN_EPS = 1e-5
NEG_BIG = -1e30
LANES = 128
ROW_TILE = 256
EXPERT_TILE = 512
VMEM_LIMIT = 56 * 1024 * 1024

F32 = jnp.float32
BF16 = jnp.bfloat16


def _round_up(n, m):
    return -(-n // m) * m


def _pick_tile(n, limit, mult):
    best = None
    for t in range(mult, min(n, limit) + 1, mult):
        if n % t == 0:
            best = t
    assert best is not None, (n, limit, mult)
    return best


def _params(sem):
    return pltpu.CompilerParams(dimension_semantics=sem, vmem_limit_bytes=VMEM_LIMIT)


def _layer_norm(y, g, b):
    mu = jnp.mean(y, axis=-1, keepdims=True)
    d = y - mu
    var = jnp.mean(d * d, axis=-1, keepdims=True)
    return d * lax.rsqrt(var + LN_EPS) * g + b


def _split_bf16(x):
    hi = x.astype(BF16)
    return hi, (x - hi.astype(F32)).astype(BF16)


def _dot3(x, w):
    xh, xl = _split_bf16(x)
    wh, wl = _split_bf16(w)
    m = x.shape[0]
    r = jnp.dot(jnp.concatenate([xh, xl], axis=0), wh, preferred_element_type=F32)
    return r[:m] + r[m:] + jnp.dot(xh, wl, preferred_element_type=F32)


def _mm_gelu_kernel(x_ref, w_ref, b_ref, o_ref, *scratch, hi, gelu):
    if hi:
        h = _dot3(x_ref[...], w_ref[...])
    else:
        xb_ref, = scratch

        @pl.when(pl.program_id(1) == 0)
        def _():
            xb_ref[...] = x_ref[...].astype(BF16)

        h = jnp.dot(xb_ref[...], w_ref[...].astype(BF16), preferred_element_type=F32)
    h = h + b_ref[...]
    o_ref[...] = 0.5 * h * (1.0 + lax.erf(h * (2.0 ** -0.5))) if gelu else h


def _mm_bias(x, w, b, *, gelu, hi=False):
    m, k = x.shape
    n = w.shape[1]
    tm = _pick_tile(m, 1024, 8)
    tn = _pick_tile(n, 512, LANES)
    return pl.pallas_call(
        functools.partial(_mm_gelu_kernel, hi=hi, gelu=gelu),
        grid=(m // tm, n // tn),
        in_specs=[pl.BlockSpec((tm, k), lambda i, j: (i, 0)),
                  pl.BlockSpec((k, tn), lambda i, j: (0, j)),
                  pl.BlockSpec((1, tn), lambda i, j: (0, j))],
        out_specs=pl.BlockSpec((tm, tn), lambda i, j: (i, j)),
        out_shape=jax.ShapeDtypeStruct((m, n), F32),
        scratch_shapes=[] if hi else [pltpu.VMEM((tm, k), BF16)],
        compiler_params=_params(("arbitrary", "arbitrary")),
        name="mm_bias_hi" if hi else "mm_bias",
    )(x, w, b.reshape(1, n))


def _mm_resid_ln_kernel(a_ref, w_ref, x_ref, g_ref, b_ref, o_ref, acc_ref, *, alpha, hi):
    kk = pl.program_id(1)
    if hi:
        part = _dot3(a_ref[...], w_ref[...])
    else:
        part = jnp.dot(a_ref[...].astype(BF16), w_ref[...].astype(BF16), preferred_element_type=F32)

    @pl.when(kk == 0)
    def _():
        acc_ref[...] = part

    @pl.when(kk > 0)
    def _():
        acc_ref[...] += part

    @pl.when(kk == pl.num_programs(1) - 1)
    def _():
        o_ref[...] = _layer_norm(alpha * x_ref[...] + acc_ref[...], g_ref[...], b_ref[...])


def _mm_resid_ln(a, w, x, g, b, alpha, hi=False):
    m, k = a.shape
    n = w.shape[1]
    tm = _pick_tile(m, 512, 8)
    tk = _pick_tile(k, 512, LANES)
    return pl.pallas_call(
        functools.partial(_mm_resid_ln_kernel, alpha=alpha, hi=hi),
        grid=(m // tm, k // tk),
        in_specs=[pl.BlockSpec((tm, tk), lambda i, j: (i, j)),
                  pl.BlockSpec((tk, n), lambda i, j: (j, 0)),
                  pl.BlockSpec((tm, n), lambda i, j: (i, 0)),
                  pl.BlockSpec((1, n), lambda i, j: (0, 0)),
                  pl.BlockSpec((1, n), lambda i, j: (0, 0))],
        out_specs=pl.BlockSpec((tm, n), lambda i, j: (i, 0)),
        out_shape=jax.ShapeDtypeStruct((m, n), F32),
        scratch_shapes=[pltpu.VMEM((tm, n), F32)],
        compiler_params=_params(("arbitrary", "arbitrary")),
        name="mm_resid_ln_hi" if hi else "mm_resid_ln",
    )(a, w, x, g.reshape(1, n), b.reshape(1, n))


def _spatial_mix_kernel(u_ref, v_ref, ws_ref, bs_ref, g_ref, b_ref, vn_ref, o_ref, *, n_groups):
    vn = _layer_norm(v_ref[...], g_ref[...], b_ref[...])
    vn_ref[...] = vn
    vb = vn.astype(BF16)
    gd = vn.shape[1] // n_groups
    for gi in range(n_groups):
        sl = slice(gi * gd, (gi + 1) * gd)
        mixed = jnp.dot(ws_ref[gi].astype(BF16), vb[:, sl], preferred_element_type=F32)
        mixed = mixed + bs_ref[:, gi:gi + 1]
        o_ref[:, sl] = (u_ref[:, sl] * mixed).astype(BF16)


def _spatial_mix(z, ws_causal, bs_t, ln_g, ln_b):
    t, two_da = z.shape
    da = two_da // 2
    n_groups = ws_causal.shape[0]
    return pl.pallas_call(
        functools.partial(_spatial_mix_kernel, n_groups=n_groups),
        grid=(t // CHUNK,),
        in_specs=[pl.BlockSpec((CHUNK, da), lambda i: (i, 0)),
                  pl.BlockSpec((CHUNK, da), lambda i: (i, 1)),
                  pl.BlockSpec((n_groups, CHUNK, CHUNK), lambda i: (0, 0, 0)),
                  pl.BlockSpec((CHUNK, n_groups), lambda i: (0, 0)),
                  pl.BlockSpec((1, da), lambda i: (0, 0)),
                  pl.BlockSpec((1, da), lambda i: (0, 0))],
        out_specs=[pl.BlockSpec((CHUNK, da), lambda i: (i, 0)),
                   pl.BlockSpec((CHUNK, da), lambda i: (i, 0))],
        out_shape=[jax.ShapeDtypeStruct((t, da), F32), jax.ShapeDtypeStruct((t, da), BF16)],
        compiler_params=_params(("arbitrary",)),
        name="spatial_mix",
    )(z, z, ws_causal, bs_t, ln_g.reshape(1, da), ln_b.reshape(1, da))


def _sample_gate_kernel(u_ref, v_ref, w_ref, c_ref, g_ref, b_ref, vn_ref, o_ref):
    vn = _layer_norm(v_ref[...], g_ref[...], b_ref[...])
    vn_ref[...] = vn
    o_ref[...] = u_ref[...] * (vn * w_ref[...] + c_ref[...])


def _sample_gate(z, w00, b0, ln_g, ln_b):
    m, two_da = z.shape
    da = two_da // 2
    row = pl.BlockSpec((1, da), lambda i: (0, 0))
    return pl.pallas_call(
        _sample_gate_kernel,
        grid=(1,),
        in_specs=[pl.BlockSpec((m, da), lambda i: (0, 0)), pl.BlockSpec((m, da), lambda i: (0, 1)),
                  row, row, row, row],
        out_specs=[pl.BlockSpec((m, da), lambda i: (0, 0)), pl.BlockSpec((m, da), lambda i: (0, 0))],
        out_shape=[jax.ShapeDtypeStruct((m, da), F32), jax.ShapeDtypeStruct((m, da), F32)],
        compiler_params=_params(("arbitrary",)),
        name="sample_gate",
    )(z, z, w00.reshape(1, da), b0.reshape(1, da), ln_g.reshape(1, da), ln_b.reshape(1, da))


def _block_ranks(g, n_valid, n_blocks):
    blk = lax.broadcasted_iota(jnp.int32, g.shape, 0)
    past = blk < n_valid
    rows = []
    for n in range(n_blocks):
        gn = g[n:n + 1, :]
        beats = jnp.where(past & ((g > gn) | ((g == gn) & (blk < n))), 1.0, 0.0)
        rows.append(jnp.sum(beats, axis=0, keepdims=True))
    return jnp.concatenate(rows, axis=0)


def _moba_prompt_kernel(slopes_ref, q_ref, k_ref, v_ref, o_ref,
                        kb_ref, vt_ref, kmean_ref, bias_ref, sel_ref, acc_ref, *, n_blocks, scale):
    h = pl.program_id(1)
    qi = pl.program_id(2)
    slope = slopes_ref[h]
    blk = MOBA_BLOCK
    dh = q_ref.shape[1]
    krow = lax.broadcasted_iota(jnp.int32, (blk, blk), 0)
    qcol = lax.broadcasted_iota(jnp.int32, (blk, blk), 1)

    @pl.when(qi == 0)
    def _():
        kmean_ref[...] = jnp.zeros_like(kmean_ref)
        for n in range(n_blocks):
            kn = k_ref[n * blk:(n + 1) * blk, :]
            kb_ref[n] = kn.astype(BF16)
            vt_ref[n] = v_ref[n * blk:(n + 1) * blk, :].T.astype(BF16)
            kmean_ref[n:n + 1, :] = jnp.mean(kn, axis=0, keepdims=True)
        bias_ref[...] = slope * (qcol - krow).astype(F32)

    q_t = q_ref[...].T
    gate = jnp.dot(kmean_ref[...], q_t, precision=lax.Precision.HIGHEST, preferred_element_type=F32)
    rank = _block_ranks(gate, qi, n_blocks)
    bidx = lax.broadcasted_iota(jnp.int32, gate.shape, 0)
    chosen = (rank[0:n_blocks] < MOBA_TOPK) & (bidx[0:n_blocks] < qi)
    sel_ref[0:n_blocks, :] = (jnp.where(chosen, 0.0, NEG_BIG)
                              - slope * ((qi - bidx[0:n_blocks]) * blk).astype(F32))
    qb = (q_t * scale).astype(BF16)

    s = jnp.dot(kb_ref[qi], qb, preferred_element_type=F32) - bias_ref[...]
    s = jnp.where(krow <= qcol, s, NEG_BIG)
    m0 = jnp.max(s, axis=0, keepdims=True)
    p = jnp.exp(s - m0)
    l0 = jnp.sum(p, axis=0, keepdims=True)
    acc_ref[...] = jnp.dot(vt_ref[qi], p.astype(BF16), preferred_element_type=F32)

    def body(j, carry):
        m_old, l_old = carry
        s = jnp.dot(kb_ref[j], qb, preferred_element_type=F32) - bias_ref[...] + sel_ref[pl.ds(j, 1), :]
        m_new = jnp.maximum(m_old, jnp.max(s, axis=0, keepdims=True))
        a = jnp.exp(m_old - m_new)
        p = jnp.exp(s - m_new)
        acc_ref[...] = a * acc_ref[...] + jnp.dot(vt_ref[j], p.astype(BF16), preferred_element_type=F32)
        return m_new, a * l_old + jnp.sum(p, axis=0, keepdims=True)

    _, l_fin = lax.fori_loop(0, qi, body, (m0, l0))
    o_ref[...] = (acc_ref[...] / l_fin).T


def _moba_prompt(qkv, slopes, bp, sp, n_heads):
    d = qkv.shape[1] // 3
    dh = d // n_heads
    assert sp % MOBA_BLOCK == 0 and dh % LANES == 0
    nq = sp // MOBA_BLOCK
    nq_pad = _round_up(nq, 8)
    return pl.pallas_call(
        functools.partial(_moba_prompt_kernel, n_blocks=nq, scale=dh ** -0.5),
        grid=(bp, n_heads, nq),
        in_specs=[pl.BlockSpec(memory_space=pltpu.SMEM),
                  pl.BlockSpec((MOBA_BLOCK, dh), lambda b, h, i: (b * nq + i, h)),
                  pl.BlockSpec((sp, dh), lambda b, h, i: (b, n_heads + h)),
                  pl.BlockSpec((sp, dh), lambda b, h, i: (b, 2 * n_heads + h))],
        out_specs=pl.BlockSpec((MOBA_BLOCK, dh), lambda b, h, i: (b * nq + i, h)),
        out_shape=jax.ShapeDtypeStruct((bp * sp, d), F32),
        scratch_shapes=[pltpu.VMEM((nq, MOBA_BLOCK, dh), BF16),
                        pltpu.VMEM((nq, dh, MOBA_BLOCK), BF16),
                        pltpu.VMEM((nq_pad, dh), F32),
                        pltpu.VMEM((MOBA_BLOCK, MOBA_BLOCK), F32),
                        pltpu.VMEM((nq_pad, MOBA_BLOCK), F32),
                        pltpu.VMEM((dh, MOBA_BLOCK), F32)],
        compiler_params=_params(("arbitrary", "arbitrary", "arbitrary")),
        name="moba_prompt",
    )(slopes, qkv, qkv, qkv)


def _page_sum_kernel(pt_ref, *refs):
    del pt_ref
    *page_refs, o_ref = refs
    s = jnp.sum(page_refs[0][...], axis=0)
    for page_ref in page_refs[1:]:
        s = s + jnp.sum(page_ref[...], axis=0)
    o_ref[...] = s


def _past_block_sums(cache, layer, page_table):
    _, _, page, n_heads, dh = cache.shape
    bs, n_pages = page_table.shape
    ppb = MOBA_BLOCK // page
    nblk = n_pages // ppb

    def page_spec(u):
        return pl.BlockSpec((None, None, page, n_heads, dh),
                            lambda b, r, pt: (layer, pt[b * n_pages + r * ppb + u], 0, 0, 0))

    grid_spec = pltpu.PrefetchScalarGridSpec(
        num_scalar_prefetch=1,
        grid=(bs, nblk),
        in_specs=[page_spec(u) for u in range(ppb)],
        out_specs=pl.BlockSpec((None, None, n_heads, dh), lambda b, r, pt: (b, r, 0, 0)),
    )
    return pl.pallas_call(
        _page_sum_kernel,
        grid_spec=grid_spec,
        out_shape=jax.ShapeDtypeStruct((bs, nblk, n_heads, dh), F32),
        compiler_params=_params(("arbitrary", "arbitrary")),
        name="page_sum",
    )(page_table.reshape(-1), *([cache] * ppb))


def _sample_select_kernel(q_ref, ksum_ref, seg_ref, o_ref, *, n_blocks):
    prod = ksum_ref[...] * (1.0 / MOBA_BLOCK) * q_ref[...]
    gate = jnp.dot(prod, seg_ref[...], precision=lax.Precision.HIGHEST, preferred_element_type=F32)
    rank = _block_ranks(gate, n_blocks, n_blocks)
    bidx = lax.broadcasted_iota(jnp.int32, gate.shape, 0).astype(F32)
    rows = []
    for r in range(MOBA_TOPK):
        rows.append(jnp.sum(jnp.where(rank == float(r), bidx, 0.0), axis=0, keepdims=True))
    rows.append(jnp.zeros((8 - MOBA_TOPK, gate.shape[1]), F32))
    o_ref[...] = jnp.concatenate(rows, axis=0).astype(jnp.int32)


def _sample_select(q_s, ksum, n_heads):
    bs, nblk, _, dh = ksum.shape
    hd = n_heads * dh
    ksum = ksum.reshape(bs, nblk, hd)
    assert nblk >= MOBA_TOPK and nblk % 8 == 0 and n_heads <= LANES
    seg = (jnp.arange(hd)[:, None] // dh == jnp.arange(LANES)[None, :]).astype(F32)
    out = pl.pallas_call(
        functools.partial(_sample_select_kernel, n_blocks=nblk),
        grid=(bs,),
        in_specs=[pl.BlockSpec((None, 1, hd), lambda b: (b, 0, 0)),
                  pl.BlockSpec((None, nblk, hd), lambda b: (b, 0, 0)),
                  pl.BlockSpec((hd, LANES), lambda b: (0, 0))],
        out_specs=pl.BlockSpec((None, 8, LANES), lambda b: (b, 0, 0)),
        out_shape=jax.ShapeDtypeStruct((bs, 8, LANES), jnp.int32),
        compiler_params=_params(("arbitrary",)),
        name="sample_select",
    )(q_s.reshape(bs, 1, hd), ksum, seg)
    return out[:, :MOBA_TOPK, :n_heads].transpose(0, 2, 1)


def _sample_attend_kernel(pt_ref, sel_ref, slopes_ref, q_ref, kn_ref, vn_ref, ck_hbm, cv_hbm, o_ref,
                          kbuf, vbuf, sem, *, layer, n_heads, n_pages, pages_per_block, scale):
    b = pl.program_id(0)
    page, dh = kbuf.shape[2], kbuf.shape[3]
    n_sub = MOBA_TOPK * pages_per_block
    past_len = n_pages * page
    hi = lax.Precision.HIGHEST

    def block_of(h, s):
        return sel_ref[(b * n_heads + h) * MOBA_TOPK + s // pages_per_block]

    def page_copies(h, s):
        pg = pt_ref[b * n_pages + block_of(h, s) * pages_per_block + s % pages_per_block]
        return (pltpu.make_async_copy(ck_hbm.at[layer, pg, :, h, :], kbuf.at[h, s], sem.at[0, h]),
                pltpu.make_async_copy(cv_hbm.at[layer, pg, :, h, :], vbuf.at[h, s], sem.at[1, h]))

    for h in range(n_heads):
        for s in range(n_sub):
            for cp in page_copies(h, s):
                cp.start()

    lane = lax.broadcasted_iota(jnp.int32, (1, n_sub * page), 1)
    for h in range(n_heads):
        for s in range(n_sub):
            for cp in page_copies(h, s):
                cp.wait()
        q = q_ref[h:h + 1, :]
        q8 = jnp.broadcast_to(q, (8, dh))
        kh = kbuf[h].reshape(n_sub * page, dh)
        vh = vbuf[h].reshape(n_sub * page, dh)
        sc = lax.dot_general(q8, kh, (((1,), (1,)), ((), ())), precision=hi,
                             preferred_element_type=F32)[0:1, :] * scale
        pos = jnp.zeros((1, n_sub * page), jnp.int32)
        for s in range(n_sub):
            first = block_of(h, s) * MOBA_BLOCK + (s % pages_per_block) * page
            pos = jnp.where((lane >= s * page) & (lane < (s + 1) * page), first + (lane - s * page), pos)
        sc = sc - slopes_ref[h] * (past_len - pos).astype(F32)
        s_new = jnp.sum(q * kn_ref[h:h + 1, :], axis=-1, keepdims=True) * scale
        m = jnp.maximum(jnp.max(sc, axis=-1, keepdims=True), s_new)
        p = jnp.exp(sc - m)
        p_new = jnp.exp(s_new - m)
        denom = jnp.sum(p, axis=-1, keepdims=True) + p_new
        pv = jnp.dot(jnp.broadcast_to(p, (8, n_sub * page)), vh, precision=hi, preferred_element_type=F32)
        o_ref[h:h + 1, :] = (pv[0:1, :] + p_new * vn_ref[h:h + 1, :]) / denom


def _sample_attend(q_s, k_new, v_new, cache_k, cache_v, layer, page_table, sel, slopes, n_heads):
    bs, hd = q_s.shape
    dh = hd // n_heads
    page = cache_k.shape[2]
    n_pages = page_table.shape[1]
    ppb = MOBA_BLOCK // page
    n_sub = MOBA_TOPK * ppb
    row_spec = pl.BlockSpec((None, n_heads, dh), lambda b, pt, sl: (b, 0, 0))
    grid_spec = pltpu.PrefetchScalarGridSpec(
        num_scalar_prefetch=2,
        grid=(bs,),
        in_specs=[pl.BlockSpec(memory_space=pltpu.SMEM),
                  row_spec, row_spec, row_spec,
                  pl.BlockSpec(memory_space=pl.ANY),
                  pl.BlockSpec(memory_space=pl.ANY)],
        out_specs=pl.BlockSpec((None, n_heads, dh), lambda b, pt, sl: (b, 0, 0)),
        scratch_shapes=[pltpu.VMEM((n_heads, n_sub, page, dh), F32),
                        pltpu.VMEM((n_heads, n_sub, page, dh), F32),
                        pltpu.SemaphoreType.DMA((2, n_heads))],
    )
    out = pl.pallas_call(
        functools.partial(_sample_attend_kernel, layer=layer, n_heads=n_heads, n_pages=n_pages,
                          pages_per_block=ppb, scale=dh ** -0.5),
        grid_spec=grid_spec,
        out_shape=jax.ShapeDtypeStruct((bs, n_heads, dh), F32),
        compiler_params=_params(("arbitrary",)),
        name="sample_attend",
    )(page_table.reshape(-1), sel.reshape(-1), slopes,
      q_s.reshape(bs, n_heads, dh), k_new.reshape(bs, n_heads, dh), v_new.reshape(bs, n_heads, dh),
      cache_k, cache_v)
    return out.reshape(bs, hd)


def _router_kernel(x_ref, w_ref, b_ref, slab_ref, cnt_ref, carry_ref, *, n_tokens):
    i = pl.program_id(0)
    tm = x_ref.shape[0]

    @pl.when(i == 0)
    def _():
        carry_ref[...] = jnp.zeros_like(carry_ref)

    logits = jnp.dot(x_ref[...], w_ref[...], precision=lax.Precision.HIGHEST,
                     preferred_element_type=F32) + b_ref[...]
    lane = lax.broadcasted_iota(jnp.int32, logits.shape, 1)
    lane_f = lane.astype(F32)
    valid = (lax.broadcasted_iota(jnp.int32, (tm, 1), 0) + i * tm) < n_tokens
    vals, onehots = [], []
    work = logits
    for _ in range(TOP_K):
        mx = jnp.max(work, axis=-1, keepdims=True)
        idx = jnp.min(jnp.where(work == mx, lane_f, float(LANES)), axis=-1, keepdims=True)
        oh = lane_f == idx
        vals.append(mx)
        onehots.append(jnp.where(oh & valid, 1.0, 0.0))
        work = jnp.where(oh, -3e38, work)
    exps = [jnp.exp(v - vals[0]) for v in vals]
    denom = exps[0]
    for e in exps[1:]:
        denom = denom + e
    oh_all = onehots[0]
    for oh in onehots[1:]:
        oh_all = oh_all + oh
    r = lax.broadcasted_iota(jnp.int32, (tm, tm), 0)
    c = lax.broadcasted_iota(jnp.int32, (tm, tm), 1)
    tri = jnp.where(c < r, 1.0, 0.0).astype(BF16)
    before = jnp.dot(tri, oh_all.astype(BF16), preferred_element_type=F32) + carry_ref[...]
    slab = jnp.zeros(logits.shape, F32)
    for kk in range(TOP_K):
        idx_k = jnp.sum(onehots[kk] * lane_f, axis=-1, keepdims=True)
        rank_k = jnp.sum(onehots[kk] * before, axis=-1, keepdims=True)
        gate_k = jnp.where(valid, exps[kk] / denom, 0.0)
        slab = jnp.where(lane == kk, idx_k, slab)
        slab = jnp.where(lane == TOP_K + kk, gate_k, slab)
        slab = jnp.where(lane == 2 * TOP_K + kk, rank_k, slab)
    slab_ref[...] = slab
    carry_ref[...] += jnp.sum(oh_all, axis=0, keepdims=True)
    cnt_ref[...] = jnp.broadcast_to(carry_ref[...], cnt_ref.shape)


def _router(x, w, b, n_tokens):
    t, d = x.shape
    n_exp = w.shape[1]
    assert TOP_K <= n_exp <= LANES and 3 * TOP_K <= LANES
    w_pad = jnp.zeros((d, LANES), F32).at[:, :n_exp].set(w)
    b_pad = jnp.full((1, LANES), NEG_BIG, F32).at[0, :n_exp].set(b)
    return pl.pallas_call(
        functools.partial(_router_kernel, n_tokens=n_tokens),
        grid=(t // ROW_TILE,),
        in_specs=[pl.BlockSpec((ROW_TILE, d), lambda i: (i, 0)),
                  pl.BlockSpec((d, LANES), lambda i: (0, 0)),
                  pl.BlockSpec((1, LANES), lambda i: (0, 0))],
        out_specs=[pl.BlockSpec((ROW_TILE, LANES), lambda i: (i, 0)),
                   pl.BlockSpec((8, LANES), lambda i: (0, 0))],
        out_shape=[jax.ShapeDtypeStruct((t, LANES), F32), jax.ShapeDtypeStruct((8, LANES), F32)],
        scratch_shapes=[pltpu.VMEM((1, LANES), F32)],
        compiler_params=_params(("arbitrary",)),
        name="router",
    )(x, w_pad, b_pad)


def _gather_rows_kernel(idx_ref, nact_ref, x_hbm, o_ref, sem):
    i = pl.program_id(0)
    rows = o_ref.shape[0]

    def row_copy(r, src_row):
        return pltpu.make_async_copy(x_hbm.at[pl.ds(src_row, 1), :], o_ref.at[pl.ds(r, 1), :], sem)

    @pl.when(i < nact_ref[0])
    def _():
        def start(r, carry):
            row_copy(r, idx_ref[i * rows + r]).start()
            return carry

        lax.fori_loop(0, rows, start, 0)

        def wait(r, carry):
            row_copy(r, 0).wait()
            return carry

        lax.fori_loop(0, rows, wait, 0)

    @pl.when(i >= nact_ref[0])
    def _():
        o_ref[...] = jnp.zeros_like(o_ref)


def _gather_rows(x, idx, n_active_tiles, rows_per_tile):
    n = idx.shape[0]
    d = x.shape[1]
    grid_spec = pltpu.PrefetchScalarGridSpec(
        num_scalar_prefetch=2,
        grid=(n // rows_per_tile,),
        in_specs=[pl.BlockSpec(memory_space=pl.ANY)],
        out_specs=pl.BlockSpec((rows_per_tile, d), lambda i, ix, na: (i, 0)),
        scratch_shapes=[pltpu.SemaphoreType.DMA(())],
    )
    return pl.pallas_call(
        _gather_rows_kernel,
        grid_spec=grid_spec,
        out_shape=jax.ShapeDtypeStruct((n, d), x.dtype),
        compiler_params=_params(("arbitrary",)),
        name="gather_rows",
    )(idx, n_active_tiles, x)


def _expert_kernel(te_ref, nact_ref, x_ref, wg_ref, wu_ref, wd_ref, bg_ref, bu_ref, bd_ref, o_ref, xb_ref):
    del te_ref
    i = pl.program_id(0)
    c = pl.program_id(1)
    active = i < nact_ref[0]

    @pl.when(active & (c == 0))
    def _():
        xb_ref[...] = x_ref[...].astype(BF16)

    @pl.when(active)
    def _():
        xb = xb_ref[...]
        gate = jnp.dot(xb, wg_ref[...].astype(BF16), preferred_element_type=F32) + bg_ref[...]
        up = jnp.dot(xb, wu_ref[...].astype(BF16), preferred_element_type=F32) + bu_ref[...]
        gate = jnp.minimum(gate, SWIGLU_LIMIT)
        up = jnp.clip(up, -SWIGLU_LIMIT, SWIGLU_LIMIT)
        act = gate * (1.0 / (1.0 + jnp.exp(-SWIGLU_ALPHA * gate))) * (up + 1.0)
        part = jnp.dot(act.astype(BF16), wd_ref[...].astype(BF16), preferred_element_type=F32)

        @pl.when(c == 0)
        def _():
            o_ref[...] = part + bd_ref[...]

        @pl.when(c > 0)
        def _():
            o_ref[...] += part

    @pl.when(jnp.logical_not(active) & (c == 0))
    def _():
        o_ref[...] = jnp.zeros_like(o_ref)


def _experts(xs, tile_expert, n_active, w_gate_up, b_gate_up, w_down, b_down, layer):
    p, d = xs.shape
    n_exp, _, two_de = w_gate_up.shape[1:]
    de = two_de // 2
    tn = _pick_tile(de, 256, LANES)
    nc = de // tn
    bgu = b_gate_up.reshape(b_gate_up.shape[0], n_exp, 1, two_de)
    bdn = b_down.reshape(b_down.shape[0], n_exp, 1, d)
    grid_spec = pltpu.PrefetchScalarGridSpec(
        num_scalar_prefetch=2,
        grid=(p // EXPERT_TILE, nc),
        in_specs=[pl.BlockSpec((EXPERT_TILE, d), lambda i, c, te, na: (i, 0)),
                  pl.BlockSpec((None, None, d, tn), lambda i, c, te, na: (layer, te[i], 0, c)),
                  pl.BlockSpec((None, None, d, tn), lambda i, c, te, na: (layer, te[i], 0, nc + c)),
                  pl.BlockSpec((None, None, tn, d), lambda i, c, te, na: (layer, te[i], c, 0)),
                  pl.BlockSpec((None, None, 1, tn), lambda i, c, te, na: (layer, te[i], 0, c)),
                  pl.BlockSpec((None, None, 1, tn), lambda i, c, te, na: (layer, te[i], 0, nc + c)),
                  pl.BlockSpec((None, None, 1, d), lambda i, c, te, na: (layer, te[i], 0, 0))],
        out_specs=pl.BlockSpec((EXPERT_TILE, d), lambda i, c, te, na: (i, 0)),
        scratch_shapes=[pltpu.VMEM((EXPERT_TILE, d), BF16)],
    )
    return pl.pallas_call(
        _expert_kernel,
        grid_spec=grid_spec,
        out_shape=jax.ShapeDtypeStruct((p, d), F32),
        compiler_params=_params(("arbitrary", "arbitrary")),
        name="experts",
    )(tile_expert, n_active, xs, w_gate_up, w_gate_up, w_down, bgu, bgu, bdn)


def _sample_expert_kernel(el_ref, nu_ref, x_ref, comb_ref, wg_ref, wu_ref, wd_ref, bg_ref, bu_ref, bd_ref,
                          g_ref, b_ref, o_ref, acc_ref, *, alpha):
    del el_ref
    j = pl.program_id(0)
    c = pl.program_id(1)

    @pl.when((j == 0) & (c == 0))
    def _():
        acc_ref[...] = jnp.zeros_like(acc_ref)

    @pl.when(j < nu_ref[0])
    def _():
        x = x_ref[...]
        gate = jnp.minimum(_dot3(x, wg_ref[...]) + bg_ref[...], SWIGLU_LIMIT)
        up = jnp.clip(_dot3(x, wu_ref[...]) + bu_ref[...], -SWIGLU_LIMIT, SWIGLU_LIMIT)
        act = gate * (1.0 / (1.0 + jnp.exp(-SWIGLU_ALPHA * gate))) * (up + 1.0)
        part = _dot3(act, wd_ref[...])
        part = part + jnp.where(c == 0, 1.0, 0.0) * bd_ref[...]
        acc_ref[...] += comb_ref[:, 0:1] * part

    @pl.when((j == pl.num_programs(0) - 1) & (c == pl.num_programs(1) - 1))
    def _():
        o_ref[...] = _layer_norm(alpha * x_ref[...] + acc_ref[...], g_ref[...], b_ref[...])


def _sample_moe_layer(x, layer, router_w, router_b, w_gate_up, b_gate_up, w_down, b_down, ln_g, ln_b, alpha):
    m, d = x.shape
    n_exp = router_w.shape[1]
    two_de = w_gate_up.shape[3]
    de = two_de // 2
    x_pad = jnp.zeros((ROW_TILE, d), F32).at[:m].set(x)
    slab, _ = _router(x_pad, router_w, router_b, m)
    idx = slab[:m, :TOP_K].astype(jnp.int32)
    gates = slab[:m, TOP_K:2 * TOP_K]
    comb = jnp.einsum("tk,tke->te", gates, jax.nn.one_hot(idx, n_exp, dtype=F32))
    used = jnp.any(comb > 0.0, axis=0)
    n_used = jnp.sum(used).astype(jnp.int32)
    order = jnp.argsort(jnp.logical_not(used), stable=True).astype(jnp.int32)
    n_slots = min(n_exp, m * TOP_K)
    slots = jnp.minimum(jnp.arange(n_slots, dtype=jnp.int32), n_used - 1)
    e_list = order[slots]
    comb_b = jnp.broadcast_to(comb.T[:, :, None], (n_exp, m, LANES))
    tn = _pick_tile(de, 256, LANES)
    nc = de // tn
    bgu = b_gate_up.reshape(b_gate_up.shape[0], n_exp, 1, two_de)
    bdn = b_down.reshape(b_down.shape[0], n_exp, 1, d)

    def chunk(j, c, nu):
        return jnp.where(j < nu[0], c, nc - 1)

    row = pl.BlockSpec((1, d), lambda j, c, el, nu: (0, 0))
    grid_spec = pltpu.PrefetchScalarGridSpec(
        num_scalar_prefetch=2,
        grid=(n_slots, nc),
        in_specs=[pl.BlockSpec((m, d), lambda j, c, el, nu: (0, 0)),
                  pl.BlockSpec((None, m, LANES), lambda j, c, el, nu: (el[j], 0, 0)),
                  pl.BlockSpec((None, None, d, tn), lambda j, c, el, nu: (layer, el[j], 0, chunk(j, c, nu))),
                  pl.BlockSpec((None, None, d, tn), lambda j, c, el, nu: (layer, el[j], 0, nc + chunk(j, c, nu))),
                  pl.BlockSpec((None, None, tn, d), lambda j, c, el, nu: (layer, el[j], chunk(j, c, nu), 0)),
                  pl.BlockSpec((None, None, 1, tn), lambda j, c, el, nu: (layer, el[j], 0, chunk(j, c, nu))),
                  pl.BlockSpec((None, None, 1, tn), lambda j, c, el, nu: (layer, el[j], 0, nc + chunk(j, c, nu))),
                  pl.BlockSpec((None, None, 1, d), lambda j, c, el, nu: (layer, el[j], 0, 0)),
                  row, row],
        out_specs=pl.BlockSpec((m, d), lambda j, c, el, nu: (0, 0)),
        scratch_shapes=[pltpu.VMEM((m, d), F32)],
    )
    return pl.pallas_call(
        functools.partial(_sample_expert_kernel, alpha=alpha),
        grid_spec=grid_spec,
        out_shape=jax.ShapeDtypeStruct((m, d), F32),
        compiler_params=_params(("arbitrary", "arbitrary")),
        name="sample_experts",
    )(e_list, n_used.reshape(1), x, comb_b, w_gate_up, w_gate_up, w_down, bgu, bgu, bdn,
      ln_g.reshape(1, d), ln_b.reshape(1, d))


def _combine_ln_kernel(*refs, alpha):
    y_refs = refs[:TOP_K]
    slab_ref, x_ref, g_ref, b_ref, o_ref = refs[TOP_K:]
    f = alpha * x_ref[...]
    for kk in range(TOP_K):
        f = f + slab_ref[:, TOP_K + kk:TOP_K + kk + 1] * y_refs[kk][...]
    o_ref[...] = _layer_norm(f, g_ref[...], b_ref[...])


def _combine_ln(y4, slab, x, g, b, alpha):
    t, d = x.shape
    nt = t // ROW_TILE

    def y_spec(kk):
        return pl.BlockSpec((ROW_TILE, d), lambda i: (kk * nt + i, 0))

    return pl.pallas_call(
        functools.partial(_combine_ln_kernel, alpha=alpha),
        grid=(nt,),
        in_specs=[y_spec(kk) for kk in range(TOP_K)] + [
            pl.BlockSpec((ROW_TILE, LANES), lambda i: (i, 0)),
            pl.BlockSpec((ROW_TILE, d), lambda i: (i, 0)),
            pl.BlockSpec((1, d), lambda i: (0, 0)),
            pl.BlockSpec((1, d), lambda i: (0, 0))],
        out_specs=pl.BlockSpec((ROW_TILE, d), lambda i: (i, 0)),
        out_shape=jax.ShapeDtypeStruct((t, d), F32),
        compiler_params=_params(("arbitrary",)),
        name="combine_ln",
    )(*([y4] * TOP_K), slab, x, g.reshape(1, d), b.reshape(1, d))


def _moe_layer(x, n_tokens, layer, router_w, router_b, w_gate_up, b_gate_up, w_down, b_down, ln_g, ln_b, alpha):
    t, d = x.shape
    n_exp = router_w.shape[1]
    slab, cnt = _router(x, router_w, router_b, n_tokens)
    idx = slab[:, :TOP_K].astype(jnp.int32)
    rank = slab[:, 2 * TOP_K:3 * TOP_K].astype(jnp.int32)
    counts = cnt[0, :n_exp].astype(jnp.int32)
    n_tiles = (n_tokens * TOP_K + n_exp * (EXPERT_TILE - 1)) // EXPERT_TILE
    p_rows = n_tiles * EXPERT_TILE
    tiles_per_exp = (counts + EXPERT_TILE - 1) // EXPERT_TILE
    tile_end = jnp.cumsum(tiles_per_exp)
    n_active = tile_end[-1]
    row_off = (tile_end - tiles_per_exp) * EXPERT_TILE
    valid = (jnp.arange(t) < n_tokens)[:, None]
    pos = jnp.where(valid, row_off[idx] + rank, 0)
    tok = jnp.broadcast_to(jnp.arange(t, dtype=jnp.int32)[:, None], pos.shape)
    src = jnp.zeros((p_rows,), jnp.int32).at[jnp.where(valid, pos, p_rows).reshape(-1)].set(
        tok.reshape(-1), mode="drop")
    tile_ids = jnp.minimum(jnp.arange(n_tiles, dtype=jnp.int32), n_active - 1)
    tile_expert = jnp.minimum(jnp.searchsorted(tile_end, tile_ids, side="right"), n_exp - 1).astype(jnp.int32)
    n_active = n_active.reshape(1).astype(jnp.int32)

    xs = _gather_rows(x, src, n_active, EXPERT_TILE)
    ys = _experts(xs, tile_expert, n_active, w_gate_up, b_gate_up, w_down, b_down, layer)
    all_tiles = jnp.full((1,), (t * TOP_K) // EXPERT_TILE, jnp.int32)
    y4 = _gather_rows(ys, pos.T.reshape(-1), all_tiles, EXPERT_TILE)
    return _combine_ln(y4, slab, x, ln_g, ln_b, alpha)


def _gmlp_prompt(x, w_in, b_in, lnv_g, lnv_b, w_s, b_s, w_out, ln_g, ln_b, alpha):
    z = _mm_bias(x, w_in, b_in, gelu=True)
    causal = jnp.tril(jnp.ones((CHUNK, CHUNK), dtype=bool))
    vn, gated = _spatial_mix(z, jnp.where(causal[None], w_s, 0.0), b_s.T, lnv_g, lnv_b)
    return _mm_resid_ln(gated, w_out, x, ln_g, ln_b, alpha), vn


def _gmlp_sample(x, w_in, b_in, lnv_g, lnv_b, w_s, b_s, w_out, ln_g, ln_b, alpha):
    z = _mm_bias(x, w_in, b_in, gelu=True, hi=True)
    group_dim = lnv_g.shape[0] // w_s.shape[0]
    vn, gated = _sample_gate(z, jnp.repeat(w_s[:, 0, 0], group_dim), jnp.repeat(b_s[:, 0], group_dim), lnv_g, lnv_b)
    return _mm_resid_ln(gated, w_out, x, ln_g, ln_b, alpha, hi=True), vn


def kernel(x_prompt, x_sample, cache_k, cache_v, page_table, w_in_a, b_in_a, ln_v_g, ln_v_b, w_s, b_s, w_out_a, w_qkv, w_o, ln_mix_g, ln_mix_b, ln_ffn_g, ln_ffn_b, router_w, router_b, w_gate_up, b_gate_up, w_down, b_down):
    bp, sp, d = x_prompt.shape
    bs, ss, _ = x_sample.shape
    depth = ln_mix_g.shape[0]
    n_heads = cache_k.shape[3]
    dh = cache_k.shape[4]
    page = cache_k.shape[2]
    n_pages = page_table.shape[1]
    assert ss == 1 and sp % CHUNK == 0 and sp % MOBA_BLOCK == 0
    assert MOBA_BLOCK % page == 0 and (n_pages * page) % MOBA_BLOCK == 0
    alpha = (2 * depth) ** 0.25
    n_prompt = bp * sp
    n_tokens = n_prompt + bs
    slopes = 2.0 ** (-8.0 * jnp.arange(1, n_heads + 1, dtype=F32) / n_heads)
    no_bias = jnp.zeros((3 * d,), F32)

    xp = x_prompt.reshape(n_prompt, d)
    xs = x_sample.reshape(bs, d)
    v_rows_p, v_rows_s, k_p, v_p, k_s, v_s = [], [], [], [], [], []
    for i in range(depth):
        li = i // 2
        g_mix, b_mix = ln_mix_g[i], ln_mix_b[i]
        if i % 2 == 0:
            mixer = (w_in_a[li], b_in_a[li], ln_v_g[li], ln_v_b[li], w_s[li], b_s[li], w_out_a[li], g_mix, b_mix, alpha)
            xp, vn_p = _gmlp_prompt(xp, *mixer)
            xs, vn_s = _gmlp_sample(xs, *mixer)
            last0 = ((sp - 1) // CHUNK) * CHUNK
            v_rows_p.append(vn_p.reshape(bp, sp, -1)[:, last0:])
            v_rows_s.append(vn_s.reshape(bs, ss, -1))
        else:
            qkv = _mm_bias(xp, w_qkv[li], no_bias, gelu=False)
            attn_p = _moba_prompt(qkv, slopes, bp, sp, n_heads)
            xp = _mm_resid_ln(attn_p, w_o[li], xp, g_mix, b_mix, alpha)
            k_p.append(qkv[:, d:2 * d].reshape(bp, sp, n_heads, dh))
            v_p.append(qkv[:, 2 * d:].reshape(bp, sp, n_heads, dh))

            qkv_s = _mm_bias(xs, w_qkv[li], no_bias, gelu=False, hi=True)
            q_s, k_new, v_new = qkv_s[:, :d], qkv_s[:, d:2 * d], qkv_s[:, 2 * d:]
            ksum = _past_block_sums(cache_k, li, page_table)
            sel = _sample_select(q_s, ksum, n_heads)
            attn_s = _sample_attend(q_s, k_new, v_new, cache_k, cache_v, li, page_table, sel, slopes, n_heads)
            xs = _mm_resid_ln(attn_s, w_o[li], xs, g_mix, b_mix, alpha, hi=True)
            k_s.append(k_new.reshape(bs, ss, n_heads, dh))
            v_s.append(v_new.reshape(bs, ss, n_heads, dh))
        moe = (router_w[i], router_b[i], w_gate_up, b_gate_up, w_down, b_down, ln_ffn_g[i], ln_ffn_b[i], alpha)
        if i < depth - 1:
            xp = _moe_layer(xp, n_prompt, i, *moe)
            xs = _sample_moe_layer(xs, i, *moe)
        else:
            t = _round_up(n_tokens, ROW_TILE)
            x = _moe_layer(jnp.concatenate([xp, xs, jnp.zeros((t - n_tokens, d), F32)], axis=0), n_tokens, i, *moe)
            xp, xs = x[:n_prompt], x[n_prompt:n_tokens]
    return (xp.reshape(bp, sp, d), xs.reshape(bs, ss, d), jnp.stack(v_rows_p), jnp.stack(v_rows_s),
            jnp.stack(k_p), jnp.stack(v_p), jnp.stack(k_s), jnp.stack(v_s))
```

```python
import functools

import jax
import jax.numpy as jnp
from jax import lax
from jax.experimental import pallas as pl
from jax.experimental.pallas import tpu as pltpu

CHUNK = 128
MOBA_BLOCK = 256
MOBA_TOPK = 3
TOP_K = 4
SWIGLU_LIMIT = 7.0
SWIGLU_ALPHA = 1.702
LN_EPS = 1e-5
NEG_BIG = -1e30
LANES = 128
ROW_TILE = 256
SUPER_TILE = 1024
SUB_TILE = 256
VMEM_LIMIT = 56 * 1024 * 1024

F32 = jnp.float32
BF16 = jnp.bfloat16


def _round_up(n, m):
    return -(-n // m) * m


def _pick_tile(n, limit, mult):
    best = None
    for t in range(mult, min(n, limit) + 1, mult):
        if n % t == 0:
            best = t
    assert best is not None, (n, limit, mult)
    return best


def _params(sem):
    return pltpu.CompilerParams(dimension_semantics=sem, vmem_limit_bytes=VMEM_LIMIT)


def _layer_norm(y, g, b):
    mu = jnp.mean(y, axis=-1, keepdims=True)
    d = y - mu
    var = jnp.mean(d * d, axis=-1, keepdims=True)
    return d * lax.rsqrt(var + LN_EPS) * g + b


def _split_bf16(x):
    hi = x.astype(BF16)
    return hi, (x - hi.astype(F32)).astype(BF16)


def _dot3(x, w):
    xh, xl = _split_bf16(x)
    wh, wl = _split_bf16(w)
    m = x.shape[0]
    r = jnp.dot(jnp.concatenate([xh, xl], axis=0), wh, preferred_element_type=F32)
    return r[:m] + r[m:] + jnp.dot(xh, wl, preferred_element_type=F32)


def _mm_gelu_kernel(x_ref, w_ref, b_ref, o_ref, *scratch, hi, gelu):
    if hi:
        h = _dot3(x_ref[...], w_ref[...])
    else:
        xb_ref, = scratch

        @pl.when(pl.program_id(1) == 0)
        def _():
            xb_ref[...] = x_ref[...].astype(BF16)

        h = jnp.dot(xb_ref[...], w_ref[...].astype(BF16), preferred_element_type=F32)
    h = h + b_ref[...]
    o_ref[...] = 0.5 * h * (1.0 + lax.erf(h * (2.0 ** -0.5))) if gelu else h


def _mm_bias(x, w, b, *, gelu, hi=False):
    m, k = x.shape
    n = w.shape[1]
    tm = _pick_tile(m, 1024, 8)
    tn = _pick_tile(n, 512, LANES)
    return pl.pallas_call(
        functools.partial(_mm_gelu_kernel, hi=hi, gelu=gelu),
        grid=(m // tm, n // tn),
        in_specs=[pl.BlockSpec((tm, k), lambda i, j: (i, 0)),
                  pl.BlockSpec((k, tn), lambda i, j: (0, j)),
                  pl.BlockSpec((1, tn), lambda i, j: (0, j))],
        out_specs=pl.BlockSpec((tm, tn), lambda i, j: (i, j)),
        out_shape=jax.ShapeDtypeStruct((m, n), F32),
        scratch_shapes=[] if hi else [pltpu.VMEM((tm, k), BF16)],
        compiler_params=_params(("arbitrary", "arbitrary")),
        name="mm_bias_hi" if hi else "mm_bias",
    )(x, w, b.reshape(1, n))


def _mm_resid_ln_kernel(a_ref, w_ref, x_ref, g_ref, b_ref, o_ref, acc_ref, *, alpha, hi):
    kk = pl.program_id(1)
    if hi:
        part = _dot3(a_ref[...], w_ref[...])
    else:
        part = jnp.dot(a_ref[...].astype(BF16), w_ref[...].astype(BF16), preferred_element_type=F32)

    @pl.when(kk == 0)
    def _():
        acc_ref[...] = part

    @pl.when(kk > 0)
    def _():
        acc_ref[...] += part

    @pl.when(kk == pl.num_programs(1) - 1)
    def _():
        o_ref[...] = _layer_norm(alpha * x_ref[...] + acc_ref[...], g_ref[...], b_ref[...])


def _mm_resid_ln(a, w, x, g, b, alpha, hi=False):
    m, k = a.shape
    n = w.shape[1]
    tm = _pick_tile(m, 512, 8)
    tk = _pick_tile(k, 512, LANES)
    return pl.pallas_call(
        functools.partial(_mm_resid_ln_kernel, alpha=alpha, hi=hi),
        grid=(m // tm, k // tk),
        in_specs=[pl.BlockSpec((tm, tk), lambda i, j: (i, j)),
                  pl.BlockSpec((tk, n), lambda i, j: (j, 0)),
                  pl.BlockSpec((tm, n), lambda i, j: (i, 0)),
                  pl.BlockSpec((1, n), lambda i, j: (0, 0)),
                  pl.BlockSpec((1, n), lambda i, j: (0, 0))],
        out_specs=pl.BlockSpec((tm, n), lambda i, j: (i, 0)),
        out_shape=jax.ShapeDtypeStruct((m, n), F32),
        scratch_shapes=[pltpu.VMEM((tm, n), F32)],
        compiler_params=_params(("arbitrary", "arbitrary")),
        name="mm_resid_ln_hi" if hi else "mm_resid_ln",
    )(a, w, x, g.reshape(1, n), b.reshape(1, n))


def _spatial_mix_kernel(u_ref, v_ref, ws_ref, bs_ref, g_ref, b_ref, vn_ref, o_ref, *, n_groups):
    vn = _layer_norm(v_ref[...], g_ref[...], b_ref[...])
    vn_ref[...] = vn
    vb = vn.astype(BF16)
    gd = vn.shape[1] // n_groups
    for gi in range(n_groups):
        sl = slice(gi * gd, (gi + 1) * gd)
        mixed = jnp.dot(ws_ref[gi].astype(BF16), vb[:, sl], preferred_element_type=F32)
        mixed = mixed + bs_ref[:, gi:gi + 1]
        o_ref[:, sl] = (u_ref[:, sl] * mixed).astype(BF16)


def _spatial_mix(z, ws_causal, bs_t, ln_g, ln_b):
    t, two_da = z.shape
    da = two_da // 2
    n_groups = ws_causal.shape[0]
    return pl.pallas_call(
        functools.partial(_spatial_mix_kernel, n_groups=n_groups),
        grid=(t // CHUNK,),
        in_specs=[pl.BlockSpec((CHUNK, da), lambda i: (i, 0)),
                  pl.BlockSpec((CHUNK, da), lambda i: (i, 1)),
                  pl.BlockSpec((n_groups, CHUNK, CHUNK), lambda i: (0, 0, 0)),
                  pl.BlockSpec((CHUNK, n_groups), lambda i: (0, 0)),
                  pl.BlockSpec((1, da), lambda i: (0, 0)),
                  pl.BlockSpec((1, da), lambda i: (0, 0))],
        out_specs=[pl.BlockSpec((CHUNK, da), lambda i: (i, 0)),
                   pl.BlockSpec((CHUNK, da), lambda i: (i, 0))],
        out_shape=[jax.ShapeDtypeStruct((t, da), F32), jax.ShapeDtypeStruct((t, da), BF16)],
        compiler_params=_params(("arbitrary",)),
        name="spatial_mix",
    )(z, z, ws_causal, bs_t, ln_g.reshape(1, da), ln_b.reshape(1, da))


def _sample_gate_kernel(u_ref, v_ref, w_ref, c_ref, g_ref, b_ref, vn_ref, o_ref):
    vn = _layer_norm(v_ref[...], g_ref[...], b_ref[...])
    vn_ref[...] = vn
    o_ref[...] = u_ref[...] * (vn * w_ref[...] + c_ref[...])


def _sample_gate(z, w00, b0, ln_g, ln_b):
    m, two_da = z.shape
    da = two_da // 2
    row = pl.BlockSpec((1, da), lambda i: (0, 0))
    return pl.pallas_call(
        _sample_gate_kernel,
        grid=(1,),
        in_specs=[pl.BlockSpec((m, da), lambda i: (0, 0)), pl.BlockSpec((m, da), lambda i: (0, 1)),
                  row, row, row, row],
        out_specs=[pl.BlockSpec((m, da), lambda i: (0, 0)), pl.BlockSpec((m, da), lambda i: (0, 0))],
        out_shape=[jax.ShapeDtypeStruct((m, da), F32), jax.ShapeDtypeStruct((m, da), F32)],
        compiler_params=_params(("arbitrary",)),
        name="sample_gate",
    )(z, z, w00.reshape(1, da), b0.reshape(1, da), ln_g.reshape(1, da), ln_b.reshape(1, da))


def _block_ranks(g, n_valid, n_blocks):
    blk = lax.broadcasted_iota(jnp.int32, g.shape, 0)
    past = blk < n_valid
    rows = []
    for n in range(n_blocks):
        gn = g[n:n + 1, :]
        beats = jnp.where(past & ((g > gn) | ((g == gn) & (blk < n))), 1.0, 0.0)
        rows.append(jnp.sum(beats, axis=0, keepdims=True))
    return jnp.concatenate(rows, axis=0)


def _moba_prompt_kernel(slopes_ref, q_ref, k_ref, v_ref, o_ref,
                        kb_ref, vt_ref, kmean_ref, bias_ref, sel_ref, acc_ref, *, n_blocks, scale):
    h = pl.program_id(1)
    qi = pl.program_id(2)
    slope = slopes_ref[h]
    blk = MOBA_BLOCK
    dh = q_ref.shape[1]
    krow = lax.broadcasted_iota(jnp.int32, (blk, blk), 0)
    qcol = lax.broadcasted_iota(jnp.int32, (blk, blk), 1)

    @pl.when(qi == 0)
    def _():
        kmean_ref[...] = jnp.zeros_like(kmean_ref)
        for n in range(n_blocks):
            kn = k_ref[n * blk:(n + 1) * blk, :]
            kb_ref[n] = kn.astype(BF16)
            vt_ref[n] = v_ref[n * blk:(n + 1) * blk, :].T.astype(BF16)
            kmean_ref[n:n + 1, :] = jnp.mean(kn, axis=0, keepdims=True)
        bias_ref[...] = slope * (qcol - krow).astype(F32)

    q_t = q_ref[...].T
    gate = jnp.dot(kmean_ref[...], q_t, precision=lax.Precision.HIGHEST, preferred_element_type=F32)
    rank = _block_ranks(gate, qi, n_blocks)
    bidx = lax.broadcasted_iota(jnp.int32, gate.shape, 0)
    chosen = (rank[0:n_blocks] < MOBA_TOPK) & (bidx[0:n_blocks] < qi)
    sel_ref[0:n_blocks, :] = (jnp.where(chosen, 0.0, NEG_BIG)
                              - slope * ((qi - bidx[0:n_blocks]) * blk).astype(F32))
    qb = (q_t * scale).astype(BF16)

    s = jnp.dot(kb_ref[qi], qb, preferred_element_type=F32) - bias_ref[...]
    s = jnp.where(krow <= qcol, s, NEG_BIG)
    m0 = jnp.max(s, axis=0, keepdims=True)
    p = jnp.exp(s - m0)
    l0 = jnp.sum(p, axis=0, keepdims=True)
    acc_ref[...] = jnp.dot(vt_ref[qi], p.astype(BF16), preferred_element_type=F32)

    def masked_scores(j):
        return jnp.dot(kb_ref[j], qb, preferred_element_type=F32) - bias_ref[...] + sel_ref[pl.ds(j, 1), :]

    def body(i, carry):
        m_old, l_old = carry
        ja, jb = 2 * i, 2 * i + 1
        sa, sb = masked_scores(ja), masked_scores(jb)
        m_new = jnp.maximum(m_old, jnp.maximum(jnp.max(sa, axis=0, keepdims=True),
                                               jnp.max(sb, axis=0, keepdims=True)))
        a = jnp.exp(m_old - m_new)
        pa = jnp.exp(sa - m_new)
        pb = jnp.exp(sb - m_new)
        acc_ref[...] = (a * acc_ref[...] + jnp.dot(vt_ref[ja], pa.astype(BF16), preferred_element_type=F32)
                        + jnp.dot(vt_ref[jb], pb.astype(BF16), preferred_element_type=F32))
        return m_new, a * l_old + jnp.sum(pa, axis=0, keepdims=True) + jnp.sum(pb, axis=0, keepdims=True)

    _, l_fin = lax.fori_loop(0, (qi + 1) // 2, body, (m0, l0))
    o_ref[...] = (acc_ref[...] / l_fin).T


def _moba_prompt(qkv, slopes, bp, sp, n_heads):
    d = qkv.shape[1] // 3
    dh = d // n_heads
    assert sp % MOBA_BLOCK == 0 and dh % LANES == 0
    nq = sp // MOBA_BLOCK
    nq_pad = _round_up(nq, 8)
    return pl.pallas_call(
        functools.partial(_moba_prompt_kernel, n_blocks=nq, scale=dh ** -0.5),
        grid=(bp, n_heads, nq),
        in_specs=[pl.BlockSpec(memory_space=pltpu.SMEM),
                  pl.BlockSpec((MOBA_BLOCK, dh), lambda b, h, i: (b * nq + i, h)),
                  pl.BlockSpec((sp, dh), lambda b, h, i: (b, n_heads + h)),
                  pl.BlockSpec((sp, dh), lambda b, h, i: (b, 2 * n_heads + h))],
        out_specs=pl.BlockSpec((MOBA_BLOCK, dh), lambda b, h, i: (b * nq + i, h)),
        out_shape=jax.ShapeDtypeStruct((bp * sp, d), F32),
        scratch_shapes=[pltpu.VMEM((nq, MOBA_BLOCK, dh), BF16),
                        pltpu.VMEM((nq, dh, MOBA_BLOCK), BF16),
                        pltpu.VMEM((nq_pad, dh), F32),
                        pltpu.VMEM((MOBA_BLOCK, MOBA_BLOCK), F32),
                        pltpu.VMEM((nq_pad, MOBA_BLOCK), F32),
                        pltpu.VMEM((dh, MOBA_BLOCK), F32)],
        compiler_params=_params(("arbitrary", "arbitrary", "arbitrary")),
        name="moba_prompt",
    )(slopes, qkv, qkv, qkv)


def _page_sum_kernel(pt_ref, *refs):
    del pt_ref
    *page_refs, o_ref = refs
    s = jnp.sum(page_refs[0][...], axis=0)
    for page_ref in page_refs[1:]:
        s = s + jnp.sum(page_ref[...], axis=0)
    o_ref[...] = s


def _past_block_sums(cache, layer, page_table):
    _, _, page, n_heads, dh = cache.shape
    bs, n_pages = page_table.shape
    ppb = MOBA_BLOCK // page
    nblk = n_pages // ppb

    def page_spec(u):
        return pl.BlockSpec((None, None, page, n_heads, dh),
                            lambda b, r, pt: (layer, pt[b * n_pages + r * ppb + u], 0, 0, 0))

    grid_spec = pltpu.PrefetchScalarGridSpec(
        num_scalar_prefetch=1,
        grid=(bs, nblk),
        in_specs=[page_spec(u) for u in range(ppb)],
        out_specs=pl.BlockSpec((None, None, n_heads, dh), lambda b, r, pt: (b, r, 0, 0)),
    )
    return pl.pallas_call(
        _page_sum_kernel,
        grid_spec=grid_spec,
        out_shape=jax.ShapeDtypeStruct((bs, nblk, n_heads, dh), F32),
        compiler_params=_params(("arbitrary", "arbitrary")),
        name="page_sum",
    )(page_table.reshape(-1), *([cache] * ppb))


def _sample_select_kernel(q_ref, ksum_ref, seg_ref, o_ref, *, n_blocks):
    prod = ksum_ref[...] * (1.0 / MOBA_BLOCK) * q_ref[...]
    gate = jnp.dot(prod, seg_ref[...], precision=lax.Precision.HIGHEST, preferred_element_type=F32)
    rank = _block_ranks(gate, n_blocks, n_blocks)
    bidx = lax.broadcasted_iota(jnp.int32, gate.shape, 0).astype(F32)
    rows = []
    for r in range(MOBA_TOPK):
        rows.append(jnp.sum(jnp.where(rank == float(r), bidx, 0.0), axis=0, keepdims=True))
    rows.append(jnp.zeros((8 - MOBA_TOPK, gate.shape[1]), F32))
    o_ref[...] = jnp.concatenate(rows, axis=0).astype(jnp.int32)


def _sample_select(q_s, ksum, n_heads):
    bs, nblk, _, dh = ksum.shape
    hd = n_heads * dh
    ksum = ksum.reshape(bs, nblk, hd)
    assert nblk >= MOBA_TOPK and nblk % 8 == 0 and n_heads <= LANES
    seg = (jnp.arange(hd)[:, None] // dh == jnp.arange(LANES)[None, :]).astype(F32)
    out = pl.pallas_call(
        functools.partial(_sample_select_kernel, n_blocks=nblk),
        grid=(bs,),
        in_specs=[pl.BlockSpec((None, 1, hd), lambda b: (b, 0, 0)),
                  pl.BlockSpec((None, nblk, hd), lambda b: (b, 0, 0)),
                  pl.BlockSpec((hd, LANES), lambda b: (0, 0))],
        out_specs=pl.BlockSpec((None, 8, LANES), lambda b: (b, 0, 0)),
        out_shape=jax.ShapeDtypeStruct((bs, 8, LANES), jnp.int32),
        compiler_params=_params(("arbitrary",)),
        name="sample_select",
    )(q_s.reshape(bs, 1, hd), ksum, seg)
    return out[:, :MOBA_TOPK, :n_heads].transpose(0, 2, 1)


def _sample_attend_kernel(pt_ref, sel_ref, slopes_ref, q_ref, kn_ref, vn_ref, ck_hbm, cv_hbm, o_ref,
                          kbuf, vbuf, sem, *, layer, n_heads, n_pages, pages_per_block, scale):
    b = pl.program_id(0)
    page, dh = kbuf.shape[2], kbuf.shape[3]
    n_sub = MOBA_TOPK * pages_per_block
    past_len = n_pages * page
    hi = lax.Precision.HIGHEST

    def block_of(h, s):
        return sel_ref[(b * n_heads + h) * MOBA_TOPK + s // pages_per_block]

    def page_copies(h, s):
        pg = pt_ref[b * n_pages + block_of(h, s) * pages_per_block + s % pages_per_block]
        return (pltpu.make_async_copy(ck_hbm.at[layer, pg, :, h, :], kbuf.at[h, s], sem.at[0, h]),
                pltpu.make_async_copy(cv_hbm.at[layer, pg, :, h, :], vbuf.at[h, s], sem.at[1, h]))

    for h in range(n_heads):
        for s in range(n_sub):
            for cp in page_copies(h, s):
                cp.start()

    lane = lax.broadcasted_iota(jnp.int32, (1, n_sub * page), 1)
    for h in range(n_heads):
        for s in range(n_sub):
            for cp in page_copies(h, s):
                cp.wait()
        q = q_ref[h:h + 1, :]
        q8 = jnp.broadcast_to(q, (8, dh))
        kh = kbuf[h].reshape(n_sub * page, dh)
        vh = vbuf[h].reshape(n_sub * page, dh)
        sc = lax.dot_general(q8, kh, (((1,), (1,)), ((), ())), precision=hi,
                             preferred_element_type=F32)[0:1, :] * scale
        pos = jnp.zeros((1, n_sub * page), jnp.int32)
        for s in range(n_sub):
            first = block_of(h, s) * MOBA_BLOCK + (s % pages_per_block) * page
            pos = jnp.where((lane >= s * page) & (lane < (s + 1) * page), first + (lane - s * page), pos)
        sc = sc - slopes_ref[h] * (past_len - pos).astype(F32)
        s_new = jnp.sum(q * kn_ref[h:h + 1, :], axis=-1, keepdims=True) * scale
        m = jnp.maximum(jnp.max(sc, axis=-1, keepdims=True), s_new)
        p = jnp.exp(sc - m)
        p_new = jnp.exp(s_new - m)
        denom = jnp.sum(p, axis=-1, keepdims=True) + p_new
        pv = jnp.dot(jnp.broadcast_to(p, (8, n_sub * page)), vh, precision=hi, preferred_element_type=F32)
        o_ref[h:h + 1, :] = (pv[0:1, :] + p_new * vn_ref[h:h + 1, :]) / denom


def _sample_attend(q_s, k_new, v_new, cache_k, cache_v, layer, page_table, sel, slopes, n_heads):
    bs, hd = q_s.shape
    dh = hd // n_heads
    page = cache_k.shape[2]
    n_pages = page_table.shape[1]
    ppb = MOBA_BLOCK // page
    n_sub = MOBA_TOPK * ppb
    row_spec = pl.BlockSpec((None, n_heads, dh), lambda b, pt, sl: (b, 0, 0))
    grid_spec = pltpu.PrefetchScalarGridSpec(
        num_scalar_prefetch=2,
        grid=(bs,),
        in_specs=[pl.BlockSpec(memory_space=pltpu.SMEM),
                  row_spec, row_spec, row_spec,
                  pl.BlockSpec(memory_space=pl.ANY),
                  pl.BlockSpec(memory_space=pl.ANY)],
        out_specs=pl.BlockSpec((None, n_heads, dh), lambda b, pt, sl: (b, 0, 0)),
        scratch_shapes=[pltpu.VMEM((n_heads, n_sub, page, dh), F32),
                        pltpu.VMEM((n_heads, n_sub, page, dh), F32),
                        pltpu.SemaphoreType.DMA((2, n_heads))],
    )
    out = pl.pallas_call(
        functools.partial(_sample_attend_kernel, layer=layer, n_heads=n_heads, n_pages=n_pages,
                          pages_per_block=ppb, scale=dh ** -0.5),
        grid_spec=grid_spec,
        out_shape=jax.ShapeDtypeStruct((bs, n_heads, dh), F32),
        compiler_params=_params(("arbitrary",)),
        name="sample_attend",
    )(page_table.reshape(-1), sel.reshape(-1), slopes,
      q_s.reshape(bs, n_heads, dh), k_new.reshape(bs, n_heads, dh), v_new.reshape(bs, n_heads, dh),
      cache_k, cache_v)
    return out.reshape(bs, hd)


def _router_kernel(x_ref, w_ref, b_ref, slab_ref, cnt_ref, carry_ref, *, n_tokens):
    i = pl.program_id(0)
    tm = x_ref.shape[0]

    @pl.when(i == 0)
    def _():
        carry_ref[...] = jnp.zeros_like(carry_ref)

    logits = jnp.dot(x_ref[...], w_ref[...], precision=lax.Precision.HIGHEST,
                     preferred_element_type=F32) + b_ref[...]
    lane = lax.broadcasted_iota(jnp.int32, logits.shape, 1)
    lane_f = lane.astype(F32)
    valid = (lax.broadcasted_iota(jnp.int32, (tm, 1), 0) + i * tm) < n_tokens
    vals, onehots = [], []
    work = logits
    for _ in range(TOP_K):
        mx = jnp.max(work, axis=-1, keepdims=True)
        idx = jnp.min(jnp.where(work == mx, lane_f, float(LANES)), axis=-1, keepdims=True)
        oh = lane_f == idx
        vals.append(mx)
        onehots.append(jnp.where(oh & valid, 1.0, 0.0))
        work = jnp.where(oh, -3e38, work)
    exps = [jnp.exp(v - vals[0]) for v in vals]
    denom = exps[0]
    for e in exps[1:]:
        denom = denom + e
    oh_all = onehots[0]
    for oh in onehots[1:]:
        oh_all = oh_all + oh
    r = lax.broadcasted_iota(jnp.int32, (tm, tm), 0)
    c = lax.broadcasted_iota(jnp.int32, (tm, tm), 1)
    tri = jnp.where(c < r, 1.0, 0.0).astype(BF16)
    before = jnp.dot(tri, oh_all.astype(BF16), preferred_element_type=F32) + carry_ref[...]
    slab = jnp.zeros(logits.shape, F32)
    for kk in range(TOP_K):
        idx_k = jnp.sum(onehots[kk] * lane_f, axis=-1, keepdims=True)
        rank_k = jnp.sum(onehots[kk] * before, axis=-1, keepdims=True)
        gate_k = jnp.where(valid, exps[kk] / denom, 0.0)
        slab = jnp.where(lane == kk, idx_k, slab)
        slab = jnp.where(lane == TOP_K + kk, gate_k, slab)
        slab = jnp.where(lane == 2 * TOP_K + kk, rank_k, slab)
    slab_ref[...] = slab
    carry_ref[...] += jnp.sum(oh_all, axis=0, keepdims=True)
    cnt_ref[...] = jnp.broadcast_to(carry_ref[...], cnt_ref.shape)


def _router(x, w, b, n_tokens):
    t, d = x.shape
    n_exp = w.shape[1]
    assert TOP_K <= n_exp <= LANES and 3 * TOP_K <= LANES
    w_pad = jnp.zeros((d, LANES), F32).at[:, :n_exp].set(w)
    b_pad = jnp.full((1, LANES), NEG_BIG, F32).at[0, :n_exp].set(b)
    return pl.pallas_call(
        functools.partial(_router_kernel, n_tokens=n_tokens),
        grid=(t // ROW_TILE,),
        in_specs=[pl.BlockSpec((ROW_TILE, d), lambda i: (i, 0)),
                  pl.BlockSpec((d, LANES), lambda i: (0, 0)),
                  pl.BlockSpec((1, LANES), lambda i: (0, 0))],
        out_specs=[pl.BlockSpec((ROW_TILE, LANES), lambda i: (i, 0)),
                   pl.BlockSpec((8, LANES), lambda i: (0, 0))],
        out_shape=[jax.ShapeDtypeStruct((t, LANES), F32), jax.ShapeDtypeStruct((8, LANES), F32)],
        scratch_shapes=[pltpu.VMEM((1, LANES), F32)],
        compiler_params=_params(("arbitrary",)),
        name="router",
    )(x, w_pad, b_pad)


def _expert_kernel(we_ref, ws_ref, wn_ref, pair_ref, x_hbm, wg_ref, wu_ref, wd_ref, bg_ref, bu_ref, bd_ref,
                   y_hbm, xbuf, xb_ref, acc_ref, gsem, ssem, *, n_tok):
    del we_ref
    w = pl.program_id(0)
    c = pl.program_id(1)
    n_items = pl.num_programs(0)
    n_chunks = pl.num_programs(1)
    n_rows = wn_ref[w]
    slot = w % 2
    rows_per_step = -(-SUPER_TILE // (n_chunks - 1))

    def gather_copy(r, pair):
        return pltpu.make_async_copy(x_hbm.at[pl.ds(pair // TOP_K, 1), :], xbuf.at[pl.ds(r, 1), :], gsem)

    def scatter_copy(sl, r, pair):
        dst = (pair % TOP_K) * n_tok + pair // TOP_K
        return pltpu.make_async_copy(acc_ref.at[sl, pl.ds(r, 1), :], y_hbm.at[pl.ds(dst, 1), :], ssem.at[sl])

    def start_gathers(item, lo, hi):
        base = ws_ref[item]

        def body(r, carry):
            gather_copy(r, pair_ref[base + r]).start()
            return carry

        lax.fori_loop(lo, hi, body, 0)

    def wait_gathers(count):
        def body(r, carry):
            gather_copy(r, 0).wait()
            return carry

        lax.fori_loop(0, count, body, 0)

    def wait_scatters(sl, count):
        def body(r, carry):
            scatter_copy(sl, r, 0).wait()
            return carry

        lax.fori_loop(0, count, body, 0)

    @pl.when(c == 0)
    def _():
        @pl.when(w == 0)
        def _():
            xbuf[...] = jnp.zeros_like(xbuf)
            start_gathers(0, 0, n_rows)

        @pl.when(w >= 2)
        def _():
            wait_scatters(slot, wn_ref[jnp.maximum(w - 2, 0)])

        wait_gathers(n_rows)
        xb_ref[...] = xbuf[...].astype(BF16)

    @pl.when((c > 0) & (w + 1 < n_items))
    def _():
        nxt = jnp.minimum(w + 1, n_items - 1)
        n_next = wn_ref[nxt]
        start_gathers(nxt, jnp.minimum((c - 1) * rows_per_step, n_next), jnp.minimum(c * rows_per_step, n_next))

    for st in range(SUPER_TILE // SUB_TILE):
        rows = slice(st * SUB_TILE, (st + 1) * SUB_TILE)

        @pl.when(st * SUB_TILE < n_rows)
        def _():
            xb = xb_ref[rows, :]
            gate = jnp.dot(xb, wg_ref[...].astype(BF16), preferred_element_type=F32) + bg_ref[...]
            up = jnp.dot(xb, wu_ref[...].astype(BF16), preferred_element_type=F32) + bu_ref[...]
            gate = jnp.minimum(gate, SWIGLU_LIMIT)
            up = jnp.clip(up, -SWIGLU_LIMIT, SWIGLU_LIMIT)
            act = gate * (1.0 / (1.0 + jnp.exp(-SWIGLU_ALPHA * gate))) * (up + 1.0)
            part = jnp.dot(act.astype(BF16), wd_ref[...].astype(BF16), preferred_element_type=F32)

            @pl.when(c == 0)
            def _():
                acc_ref[slot, rows, :] = part + bd_ref[...]

            @pl.when(c > 0)
            def _():
                acc_ref[slot, rows, :] += part

    @pl.when(c == n_chunks - 1)
    def _():
        base = ws_ref[w]

        def body(r, carry):
            scatter_copy(slot, r, pair_ref[base + r]).start()
            return carry

        lax.fori_loop(0, n_rows, body, 0)

        @pl.when(w == n_items - 1)
        def _():
            wait_scatters(1 - slot, wn_ref[jnp.maximum(w - 1, 0)])
            wait_scatters(slot, n_rows)


def _experts(x, item_expert, item_start, item_rows, pairs, w_gate_up, b_gate_up, w_down, b_down, layer):
    t, d = x.shape
    n_exp, _, two_de = w_gate_up.shape[1:]
    de = two_de // 2
    tn = _pick_tile(de, 256, LANES)
    nc = de // tn
    assert nc >= 2
    n_items = item_expert.shape[0]
    assert n_items >= 2
    bgu = b_gate_up.reshape(b_gate_up.shape[0], n_exp, 1, two_de)
    bdn = b_down.reshape(b_down.shape[0], n_exp, 1, d)

    def chunk(i, c, wn):
        return jnp.where(wn[i] > 0, c, nc - 1)

    grid_spec = pltpu.PrefetchScalarGridSpec(
        num_scalar_prefetch=4,
        grid=(n_items, nc),
        in_specs=[pl.BlockSpec(memory_space=pl.ANY),
                  pl.BlockSpec((None, None, d, tn), lambda i, c, we, ws, wn, pr: (layer, we[i], 0, chunk(i, c, wn))),
                  pl.BlockSpec((None, None, d, tn), lambda i, c, we, ws, wn, pr: (layer, we[i], 0, nc + chunk(i, c, wn))),
                  pl.BlockSpec((None, None, tn, d), lambda i, c, we, ws, wn, pr: (layer, we[i], chunk(i, c, wn), 0)),
                  pl.BlockSpec((None, None, 1, tn), lambda i, c, we, ws, wn, pr: (layer, we[i], 0, chunk(i, c, wn))),
                  pl.BlockSpec((None, None, 1, tn), lambda i, c, we, ws, wn, pr: (layer, we[i], 0, nc + chunk(i, c, wn))),
                  pl.BlockSpec((None, None, 1, d), lambda i, c, we, ws, wn, pr: (layer, we[i], 0, 0))],
        out_specs=pl.BlockSpec(memory_space=pl.ANY),
        scratch_shapes=[pltpu.VMEM((SUPER_TILE, d), F32),
                        pltpu.VMEM((SUPER_TILE, d), BF16),
                        pltpu.VMEM((2, SUPER_TILE, d), F32),
                        pltpu.SemaphoreType.DMA(()),
                        pltpu.SemaphoreType.DMA((2,))],
    )
    return pl.pallas_call(
        functools.partial(_expert_kernel, n_tok=t),
        grid_spec=grid_spec,
        out_shape=jax.ShapeDtypeStruct((TOP_K * t, d), F32),
        compiler_params=_params(("arbitrary", "arbitrary")),
        name="experts",
    )(item_expert, item_start, item_rows, pairs, x, w_gate_up, w_gate_up, w_down, bgu, bgu, bdn)


def _sample_expert_kernel(el_ref, nu_ref, x_ref, comb_ref, wg_ref, wu_ref, wd_ref, bg_ref, bu_ref, bd_ref,
                          g_ref, b_ref, o_ref, acc_ref, *, alpha):
    del el_ref
    j = pl.program_id(0)
    c = pl.program_id(1)

    @pl.when((j == 0) & (c == 0))
    def _():
        acc_ref[...] = jnp.zeros_like(acc_ref)

    @pl.when(j < nu_ref[0])
    def _():
        x = x_ref[...]
        gate = jnp.minimum(_dot3(x, wg_ref[...]) + bg_ref[...], SWIGLU_LIMIT)
        up = jnp.clip(_dot3(x, wu_ref[...]) + bu_ref[...], -SWIGLU_LIMIT, SWIGLU_LIMIT)
        act = gate * (1.0 / (1.0 + jnp.exp(-SWIGLU_ALPHA * gate))) * (up + 1.0)
        part = _dot3(act, wd_ref[...])
        part = part + jnp.where(c == 0, 1.0, 0.0) * bd_ref[...]
        acc_ref[...] += comb_ref[:, 0:1] * part

    @pl.when((j == pl.num_programs(0) - 1) & (c == pl.num_programs(1) - 1))
    def _():
        o_ref[...] = _layer_norm(alpha * x_ref[...] + acc_ref[...], g_ref[...], b_ref[...])


def _sample_moe_layer(x, layer, router_w, router_b, w_gate_up, b_gate_up, w_down, b_down, ln_g, ln_b, alpha):
    m, d = x.shape
    n_exp = router_w.shape[1]
    two_de = w_gate_up.shape[3]
    de = two_de // 2
    x_pad = jnp.zeros((ROW_TILE, d), F32).at[:m].set(x)
    slab, _ = _router(x_pad, router_w, router_b, m)
    idx = slab[:m, :TOP_K].astype(jnp.int32)
    gates = slab[:m, TOP_K:2 * TOP_K]
    comb = jnp.einsum("tk,tke->te", gates, jax.nn.one_hot(idx, n_exp, dtype=F32))
    used = jnp.any(comb > 0.0, axis=0)
    n_used = jnp.sum(used).astype(jnp.int32)
    order = jnp.argsort(jnp.logical_not(used), stable=True).astype(jnp.int32)
    n_slots = min(n_exp, m * TOP_K)
    slots = jnp.minimum(jnp.arange(n_slots, dtype=jnp.int32), n_used - 1)
    e_list = order[slots]
    comb_b = jnp.broadcast_to(comb.T[:, :, None], (n_exp, m, LANES))
    tn = _pick_tile(de, 256, LANES)
    nc = de // tn
    bgu = b_gate_up.reshape(b_gate_up.shape[0], n_exp, 1, two_de)
    bdn = b_down.reshape(b_down.shape[0], n_exp, 1, d)

    def chunk(j, c, nu):
        return jnp.where(j < nu[0], c, nc - 1)

    row = pl.BlockSpec((1, d), lambda j, c, el, nu: (0, 0))
    grid_spec = pltpu.PrefetchScalarGridSpec(
        num_scalar_prefetch=2,
        grid=(n_slots, nc),
        in_specs=[pl.BlockSpec((m, d), lambda j, c, el, nu: (0, 0)),
                  pl.BlockSpec((None, m, LANES), lambda j, c, el, nu: (el[j], 0, 0)),
                  pl.BlockSpec((None, None, d, tn), lambda j, c, el, nu: (layer, el[j], 0, chunk(j, c, nu))),
                  pl.BlockSpec((None, None, d, tn), lambda j, c, el, nu: (layer, el[j], 0, nc + chunk(j, c, nu))),
                  pl.BlockSpec((None, None, tn, d), lambda j, c, el, nu: (layer, el[j], chunk(j, c, nu), 0)),
                  pl.BlockSpec((None, None, 1, tn), lambda j, c, el, nu: (layer, el[j], 0, chunk(j, c, nu))),
                  pl.BlockSpec((None, None, 1, tn), lambda j, c, el, nu: (layer, el[j], 0, nc + chunk(j, c, nu))),
                  pl.BlockSpec((None, None, 1, d), lambda j, c, el, nu: (layer, el[j], 0, 0)),
                  row, row],
        out_specs=pl.BlockSpec((m, d), lambda j, c, el, nu: (0, 0)),
        scratch_shapes=[pltpu.VMEM((m, d), F32)],
    )
    return pl.pallas_call(
        functools.partial(_sample_expert_kernel, alpha=alpha),
        grid_spec=grid_spec,
        out_shape=jax.ShapeDtypeStruct((m, d), F32),
        compiler_params=_params(("arbitrary", "arbitrary")),
        name="sample_experts",
    )(e_list, n_used.reshape(1), x, comb_b, w_gate_up, w_gate_up, w_down, bgu, bgu, bdn,
      ln_g.reshape(1, d), ln_b.reshape(1, d))


def _combine_ln_kernel(*refs, alpha):
    y_refs = refs[:TOP_K]
    slab_ref, x_ref, g_ref, b_ref, o_ref = refs[TOP_K:]
    f = alpha * x_ref[...]
    for kk in range(TOP_K):
        f = f + slab_ref[:, TOP_K + kk:TOP_K + kk + 1] * y_refs[kk][...]
    o_ref[...] = _layer_norm(f, g_ref[...], b_ref[...])


def _combine_ln(y4, slab, x, g, b, alpha):
    t, d = x.shape
    nt = t // ROW_TILE

    def y_spec(kk):
        return pl.BlockSpec((ROW_TILE, d), lambda i: (kk * nt + i, 0))

    return pl.pallas_call(
        functools.partial(_combine_ln_kernel, alpha=alpha),
        grid=(nt,),
        in_specs=[y_spec(kk) for kk in range(TOP_K)] + [
            pl.BlockSpec((ROW_TILE, LANES), lambda i: (i, 0)),
            pl.BlockSpec((ROW_TILE, d), lambda i: (i, 0)),
            pl.BlockSpec((1, d), lambda i: (0, 0)),
            pl.BlockSpec((1, d), lambda i: (0, 0))],
        out_specs=pl.BlockSpec((ROW_TILE, d), lambda i: (i, 0)),
        out_shape=jax.ShapeDtypeStruct((t, d), F32),
        compiler_params=_params(("arbitrary",)),
        name="combine_ln",
    )(*([y4] * TOP_K), slab, x, g.reshape(1, d), b.reshape(1, d))


def _moe_layer(x, layer, router_w, router_b, w_gate_up, b_gate_up, w_down, b_down, ln_g, ln_b, alpha):
    t, d = x.shape
    n_exp = router_w.shape[1]
    slab, cnt = _router(x, router_w, router_b, t)
    idx = slab[:, :TOP_K].astype(jnp.int32)
    rank = slab[:, 2 * TOP_K:3 * TOP_K].astype(jnp.int32)
    counts = cnt[0, :n_exp].astype(jnp.int32)
    first_row = jnp.cumsum(counts) - counts
    pos = first_row[idx] + rank
    pair_id = jnp.arange(t * TOP_K, dtype=jnp.int32).reshape(t, TOP_K)
    pairs = jnp.zeros((t * TOP_K,), jnp.int32).at[pos.reshape(-1)].set(pair_id.reshape(-1))
    n_items = (t * TOP_K) // SUPER_TILE + n_exp
    items_per_exp = (counts + SUPER_TILE - 1) // SUPER_TILE
    item_end = jnp.cumsum(items_per_exp)
    ids = jnp.arange(n_items, dtype=jnp.int32)
    live = ids < item_end[-1]
    item_expert = jnp.minimum(jnp.searchsorted(item_end, jnp.minimum(ids, item_end[-1] - 1), side="right"),
                              n_exp - 1).astype(jnp.int32)
    local = ids - (item_end - items_per_exp)[item_expert]
    item_start = jnp.where(live, first_row[item_expert] + local * SUPER_TILE, 0).astype(jnp.int32)
    item_rows = jnp.where(live, jnp.clip(counts[item_expert] - local * SUPER_TILE, 0, SUPER_TILE), 0).astype(jnp.int32)
    y4 = _experts(x, item_expert, item_start, item_rows, pairs, w_gate_up, b_gate_up, w_down, b_down, layer)
    return _combine_ln(y4, slab, x, ln_g, ln_b, alpha)


def _gmlp_prompt(x, w_in, b_in, lnv_g, lnv_b, w_s, b_s, w_out, ln_g, ln_b, alpha):
    z = _mm_bias(x, w_in, b_in, gelu=True)
    causal = jnp.tril(jnp.ones((CHUNK, CHUNK), dtype=bool))
    vn, gated = _spatial_mix(z, jnp.where(causal[None], w_s, 0.0), b_s.T, lnv_g, lnv_b)
    return _mm_resid_ln(gated, w_out, x, ln_g, ln_b, alpha), vn


def _gmlp_sample(x, w_in, b_in, lnv_g, lnv_b, w_s, b_s, w_out, ln_g, ln_b, alpha):
    z = _mm_bias(x, w_in, b_in, gelu=True, hi=True)
    group_dim = lnv_g.shape[0] // w_s.shape[0]
    vn, gated = _sample_gate(z, jnp.repeat(w_s[:, 0, 0], group_dim), jnp.repeat(b_s[:, 0], group_dim), lnv_g, lnv_b)
    return _mm_resid_ln(gated, w_out, x, ln_g, ln_b, alpha, hi=True), vn


def kernel(x_prompt, x_sample, cache_k, cache_v, page_table, w_in_a, b_in_a, ln_v_g, ln_v_b, w_s, b_s, w_out_a, w_qkv, w_o, ln_mix_g, ln_mix_b, ln_ffn_g, ln_ffn_b, router_w, router_b, w_gate_up, b_gate_up, w_down, b_down):
    bp, sp, d = x_prompt.shape
    bs, ss, _ = x_sample.shape
    depth = ln_mix_g.shape[0]
    n_heads = cache_k.shape[3]
    dh = cache_k.shape[4]
    page = cache_k.shape[2]
    n_pages = page_table.shape[1]
    assert ss == 1 and sp % CHUNK == 0 and sp % MOBA_BLOCK == 0
    assert MOBA_BLOCK % page == 0 and (n_pages * page) % MOBA_BLOCK == 0
    alpha = (2 * depth) ** 0.25
    n_prompt = bp * sp
    n_tokens = n_prompt + bs
    slopes = 2.0 ** (-8.0 * jnp.arange(1, n_heads + 1, dtype=F32) / n_heads)
    no_bias = jnp.zeros((3 * d,), F32)

    xp = x_prompt.reshape(n_prompt, d)
    xs = x_sample.reshape(bs, d)
    v_rows_p, v_rows_s, k_p, v_p, k_s, v_s = [], [], [], [], [], []
    for i in range(depth):
        li = i // 2
        g_mix, b_mix = ln_mix_g[i], ln_mix_b[i]
        if i % 2 == 0:
            mixer = (w_in_a[li], b_in_a[li], ln_v_g[li], ln_v_b[li], w_s[li], b_s[li], w_out_a[li], g_mix, b_mix, alpha)
            xp, vn_p = _gmlp_prompt(xp, *mixer)
            xs, vn_s = _gmlp_sample(xs, *mixer)
            last0 = ((sp - 1) // CHUNK) * CHUNK
            v_rows_p.append(vn_p.reshape(bp, sp, -1)[:, last0:])
            v_rows_s.append(vn_s.reshape(bs, ss, -1))
        else:
            qkv = _mm_bias(xp, w_qkv[li], no_bias, gelu=False)
            attn_p = _moba_prompt(qkv, slopes, bp, sp, n_heads)
            xp = _mm_resid_ln(attn_p, w_o[li], xp, g_mix, b_mix, alpha)
            k_p.append(qkv[:, d:2 * d].reshape(bp, sp, n_heads, dh))
            v_p.append(qkv[:, 2 * d:].reshape(bp, sp, n_heads, dh))

            qkv_s = _mm_bias(xs, w_qkv[li], no_bias, gelu=False, hi=True)
            q_s, k_new, v_new = qkv_s[:, :d], qkv_s[:, d:2 * d], qkv_s[:, 2 * d:]
            ksum = _past_block_sums(cache_k, li, page_table)
            sel = _sample_select(q_s, ksum, n_heads)
            attn_s = _sample_attend(q_s, k_new, v_new, cache_k, cache_v, li, page_table, sel, slopes, n_heads)
            xs = _mm_resid_ln(attn_s, w_o[li], xs, g_mix, b_mix, alpha, hi=True)
            k_s.append(k_new.reshape(bs, ss, n_heads, dh))
            v_s.append(v_new.reshape(bs, ss, n_heads, dh))
        moe = (router_w[i], router_b[i], w_gate_up, b_gate_up, w_down, b_down, ln_ffn_g[i], ln_ffn_b[i], alpha)
        if i < depth - 1:
            xp = _moe_layer(xp, i, *moe)
            xs = _sample_moe_layer(xs, i, *moe)
        else:
            t = _round_up(n_tokens, ROW_TILE)
            x = _moe_layer(jnp.concatenate([xp, xs, jnp.zeros((t - n_tokens, d), F32)], axis=0), i, *moe)
            xp, xs = x[:n_prompt], x[n_prompt:n_tokens]
    return (xp.reshape(bp, sp, d), xs.reshape(bs, ss, d), jnp.stack(v_rows_p), jnp.stack(v_rows_s),
            jnp.stack(k_p), jnp.stack(v_p), jnp.stack(k_s), jnp.stack(v_s))
```

```python
import functools

import jax
import jax.numpy as jnp
from jax import lax
from jax.experimental import pallas as pl
from jax.experimental.pallas import tpu as pltpu

CHUNK = 128
MOBA_BLOCK = 256
MOBA_TOPK = 3
TOP_K = 4
SWIGLU_LIMIT = 7.0
SWIGLU_ALPHA = 1.702
LN_EPS = 1e-5
NEG_BIG = -1e30
LANES = 128
ROW_TILE = 256
SUPER_TILE = 1024
SUB_TILE = 256
VMEM_LIMIT = 56 * 1024 * 1024

F32 = jnp.float32
BF16 = jnp.bfloat16


def _round_up(n, m):
    return -(-n // m) * m


def _pick_tile(n, limit, mult):
    best = None
    for t in range(mult, min(n, limit) + 1, mult):
        if n % t == 0:
            best = t
    assert best is not None, (n, limit, mult)
    return best


def _params(sem):
    return pltpu.CompilerParams(dimension_semantics=sem, vmem_limit_bytes=VMEM_LIMIT)


def _layer_norm(y, g, b):
    mu = jnp.mean(y, axis=-1, keepdims=True)
    d = y - mu
    var = jnp.mean(d * d, axis=-1, keepdims=True)
    return d * lax.rsqrt(var + LN_EPS) * g + b


def _split_bf16(x):
    hi = x.astype(BF16)
    return hi, (x - hi.astype(F32)).astype(BF16)


def _dot3(x, w):
    xh, xl = _split_bf16(x)
    wh, wl = _split_bf16(w)
    m = x.shape[0]
    r = jnp.dot(jnp.concatenate([xh, xl], axis=0), wh, preferred_element_type=F32)
    return r[:m] + r[m:] + jnp.dot(xh, wl, preferred_element_type=F32)


def _mm_gelu_kernel(x_ref, w_ref, b_ref, o_ref, *scratch, hi, gelu):
    if hi:
        h = _dot3(x_ref[...], w_ref[...])
    else:
        xb_ref, = scratch

        @pl.when(pl.program_id(1) == 0)
        def _():
            xb_ref[...] = x_ref[...].astype(BF16)

        h = jnp.dot(xb_ref[...], w_ref[...].astype(BF16), preferred_element_type=F32)
    h = h + b_ref[...]
    o_ref[...] = 0.5 * h * (1.0 + lax.erf(h * (2.0 ** -0.5))) if gelu else h


def _mm_bias(x, w, b, *, gelu, hi=False):
    m, k = x.shape
    n = w.shape[1]
    tm = _pick_tile(m, 1024, 8)
    tn = _pick_tile(n, 512, LANES)
    return pl.pallas_call(
        functools.partial(_mm_gelu_kernel, hi=hi, gelu=gelu),
        grid=(m // tm, n // tn),
        in_specs=[pl.BlockSpec((tm, k), lambda i, j: (i, 0)),
                  pl.BlockSpec((k, tn), lambda i, j: (0, j)),
                  pl.BlockSpec((1, tn), lambda i, j: (0, j))],
        out_specs=pl.BlockSpec((tm, tn), lambda i, j: (i, j)),
        out_shape=jax.ShapeDtypeStruct((m, n), F32),
        scratch_shapes=[] if hi else [pltpu.VMEM((tm, k), BF16)],
        compiler_params=_params(("arbitrary", "arbitrary")),
        name="mm_bias_hi" if hi else "mm_bias",
    )(x, w, b.reshape(1, n))


def _mm_resid_ln_kernel(a_ref, w_ref, x_ref, g_ref, b_ref, o_ref, acc_ref, *, alpha, hi):
    kk = pl.program_id(1)
    if hi:
        part = _dot3(a_ref[...], w_ref[...])
    else:
        part = jnp.dot(a_ref[...].astype(BF16), w_ref[...].astype(BF16), preferred_element_type=F32)

    @pl.when(kk == 0)
    def _():
        acc_ref[...] = part

    @pl.when(kk > 0)
    def _():
        acc_ref[...] += part

    @pl.when(kk == pl.num_programs(1) - 1)
    def _():
        o_ref[...] = _layer_norm(alpha * x_ref[...] + acc_ref[...], g_ref[...], b_ref[...])


def _mm_resid_ln(a, w, x, g, b, alpha, hi=False):
    m, k = a.shape
    n = w.shape[1]
    tm = _pick_tile(m, 512, 8)
    tk = _pick_tile(k, 512, LANES)
    return pl.pallas_call(
        functools.partial(_mm_resid_ln_kernel, alpha=alpha, hi=hi),
        grid=(m // tm, k // tk),
        in_specs=[pl.BlockSpec((tm, tk), lambda i, j: (i, j)),
                  pl.BlockSpec((tk, n), lambda i, j: (j, 0)),
                  pl.BlockSpec((tm, n), lambda i, j: (i, 0)),
                  pl.BlockSpec((1, n), lambda i, j: (0, 0)),
                  pl.BlockSpec((1, n), lambda i, j: (0, 0))],
        out_specs=pl.BlockSpec((tm, n), lambda i, j: (i, 0)),
        out_shape=jax.ShapeDtypeStruct((m, n), F32),
        scratch_shapes=[pltpu.VMEM((tm, n), F32)],
        compiler_params=_params(("arbitrary", "arbitrary")),
        name="mm_resid_ln_hi" if hi else "mm_resid_ln",
    )(a, w, x, g.reshape(1, n), b.reshape(1, n))


def _spatial_mix_kernel(u_ref, v_ref, ws_ref, bs_ref, g_ref, b_ref, vn_ref, o_ref, *, n_groups):
    vn = _layer_norm(v_ref[...], g_ref[...], b_ref[...])
    vn_ref[...] = vn
    vb = vn.astype(BF16)
    gd = vn.shape[1] // n_groups
    for gi in range(n_groups):
        sl = slice(gi * gd, (gi + 1) * gd)
        mixed = jnp.dot(ws_ref[gi].astype(BF16), vb[:, sl], preferred_element_type=F32)
        mixed = mixed + bs_ref[:, gi:gi + 1]
        o_ref[:, sl] = (u_ref[:, sl] * mixed).astype(BF16)


def _spatial_mix(z, ws_causal, bs_t, ln_g, ln_b):
    t, two_da = z.shape
    da = two_da // 2
    n_groups = ws_causal.shape[0]
    return pl.pallas_call(
        functools.partial(_spatial_mix_kernel, n_groups=n_groups),
        grid=(t // CHUNK,),
        in_specs=[pl.BlockSpec((CHUNK, da), lambda i: (i, 0)),
                  pl.BlockSpec((CHUNK, da), lambda i: (i, 1)),
                  pl.BlockSpec((n_groups, CHUNK, CHUNK), lambda i: (0, 0, 0)),
                  pl.BlockSpec((CHUNK, n_groups), lambda i: (0, 0)),
                  pl.BlockSpec((1, da), lambda i: (0, 0)),
                  pl.BlockSpec((1, da), lambda i: (0, 0))],
        out_specs=[pl.BlockSpec((CHUNK, da), lambda i: (i, 0)),
                   pl.BlockSpec((CHUNK, da), lambda i: (i, 0))],
        out_shape=[jax.ShapeDtypeStruct((t, da), F32), jax.ShapeDtypeStruct((t, da), BF16)],
        compiler_params=_params(("arbitrary",)),
        name="spatial_mix",
    )(z, z, ws_causal, bs_t, ln_g.reshape(1, da), ln_b.reshape(1, da))


def _sample_gate_kernel(u_ref, v_ref, w_ref, c_ref, g_ref, b_ref, vn_ref, o_ref):
    vn = _layer_norm(v_ref[...], g_ref[...], b_ref[...])
    vn_ref[...] = vn
    o_ref[...] = u_ref[...] * (vn * w_ref[...] + c_ref[...])


def _sample_gate(z, w00, b0, ln_g, ln_b):
    m, two_da = z.shape
    da = two_da // 2
    row = pl.BlockSpec((1, da), lambda i: (0, 0))
    return pl.pallas_call(
        _sample_gate_kernel,
        grid=(1,),
        in_specs=[pl.BlockSpec((m, da), lambda i: (0, 0)), pl.BlockSpec((m, da), lambda i: (0, 1)),
                  row, row, row, row],
        out_specs=[pl.BlockSpec((m, da), lambda i: (0, 0)), pl.BlockSpec((m, da), lambda i: (0, 0))],
        out_shape=[jax.ShapeDtypeStruct((m, da), F32), jax.ShapeDtypeStruct((m, da), F32)],
        compiler_params=_params(("arbitrary",)),
        name="sample_gate",
    )(z, z, w00.reshape(1, da), b0.reshape(1, da), ln_g.reshape(1, da), ln_b.reshape(1, da))


def _block_ranks(g, n_valid, n_blocks):
    blk = lax.broadcasted_iota(jnp.int32, g.shape, 0)
    past = blk < n_valid
    rows = []
    for n in range(n_blocks):
        gn = g[n:n + 1, :]
        beats = jnp.where(past & ((g > gn) | ((g == gn) & (blk < n))), 1.0, 0.0)
        rows.append(jnp.sum(beats, axis=0, keepdims=True))
    return jnp.concatenate(rows, axis=0)


def _moba_prompt_kernel(slopes_ref, q_ref, k_ref, v_ref, o_ref,
                        kb_ref, vt_ref, kmean_ref, bias_ref, sel_ref, acc_ref, *, n_blocks, scale):
    h = pl.program_id(1)
    qi = pl.program_id(2)
    slope = slopes_ref[h]
    blk = MOBA_BLOCK
    dh = q_ref.shape[1]
    krow = lax.broadcasted_iota(jnp.int32, (blk, blk), 0)
    qcol = lax.broadcasted_iota(jnp.int32, (blk, blk), 1)

    @pl.when(qi == 0)
    def _():
        kmean_ref[...] = jnp.zeros_like(kmean_ref)
        for n in range(n_blocks):
            kn = k_ref[n * blk:(n + 1) * blk, :]
            kb_ref[n] = kn.astype(BF16)
            vt_ref[n] = v_ref[n * blk:(n + 1) * blk, :].T.astype(BF16)
            kmean_ref[n:n + 1, :] = jnp.mean(kn, axis=0, keepdims=True)
        bias_ref[...] = slope * (qcol - krow).astype(F32)

    q_t = q_ref[...].T
    gate = jnp.dot(kmean_ref[...], q_t, precision=lax.Precision.HIGHEST, preferred_element_type=F32)
    rank = _block_ranks(gate, qi, n_blocks)
    bidx = lax.broadcasted_iota(jnp.int32, gate.shape, 0)
    chosen = (rank[0:n_blocks] < MOBA_TOPK) & (bidx[0:n_blocks] < qi)
    sel_ref[0:n_blocks, :] = (jnp.where(chosen, 0.0, NEG_BIG)
                              - slope * ((qi - bidx[0:n_blocks]) * blk).astype(F32))
    qb = (q_t * scale).astype(BF16)

    s = jnp.dot(kb_ref[qi], qb, preferred_element_type=F32) - bias_ref[...]
    s = jnp.where(krow <= qcol, s, NEG_BIG)
    m0 = jnp.max(s, axis=0, keepdims=True)
    p = jnp.exp(s - m0)
    l0 = jnp.sum(p, axis=0, keepdims=True)
    acc_ref[...] = jnp.dot(vt_ref[qi], p.astype(BF16), preferred_element_type=F32)

    def masked_scores(j):
        return jnp.dot(kb_ref[j], qb, preferred_element_type=F32) - bias_ref[...] + sel_ref[pl.ds(j, 1), :]

    def body(i, carry):
        m_old, l_old = carry
        ja, jb = 2 * i, 2 * i + 1
        sa, sb = masked_scores(ja), masked_scores(jb)
        m_new = jnp.maximum(m_old, jnp.maximum(jnp.max(sa, axis=0, keepdims=True),
                                               jnp.max(sb, axis=0, keepdims=True)))
        a = jnp.exp(m_old - m_new)
        pa = jnp.exp(sa - m_new)
        pb = jnp.exp(sb - m_new)
        acc_ref[...] = (a * acc_ref[...] + jnp.dot(vt_ref[ja], pa.astype(BF16), preferred_element_type=F32)
                        + jnp.dot(vt_ref[jb], pb.astype(BF16), preferred_element_type=F32))
        return m_new, a * l_old + jnp.sum(pa, axis=0, keepdims=True) + jnp.sum(pb, axis=0, keepdims=True)

    _, l_fin = lax.fori_loop(0, (qi + 1) // 2, body, (m0, l0))
    o_ref[...] = (acc_ref[...] / l_fin).T


def _moba_prompt(qkv, slopes, bp, sp, n_heads):
    d = qkv.shape[1] // 3
    dh = d // n_heads
    assert sp % MOBA_BLOCK == 0 and dh % LANES == 0
    nq = sp // MOBA_BLOCK
    nq_pad = _round_up(nq, 8)
    return pl.pallas_call(
        functools.partial(_moba_prompt_kernel, n_blocks=nq, scale=dh ** -0.5),
        grid=(bp, n_heads, nq),
        in_specs=[pl.BlockSpec(memory_space=pltpu.SMEM),
                  pl.BlockSpec((MOBA_BLOCK, dh), lambda b, h, i: (b * nq + i, h)),
                  pl.BlockSpec((sp, dh), lambda b, h, i: (b, n_heads + h)),
                  pl.BlockSpec((sp, dh), lambda b, h, i: (b, 2 * n_heads + h))],
        out_specs=pl.BlockSpec((MOBA_BLOCK, dh), lambda b, h, i: (b * nq + i, h)),
        out_shape=jax.ShapeDtypeStruct((bp * sp, d), F32),
        scratch_shapes=[pltpu.VMEM((nq, MOBA_BLOCK, dh), BF16),
                        pltpu.VMEM((nq, dh, MOBA_BLOCK), BF16),
                        pltpu.VMEM((nq_pad, dh), F32),
                        pltpu.VMEM((MOBA_BLOCK, MOBA_BLOCK), F32),
                        pltpu.VMEM((nq_pad, MOBA_BLOCK), F32),
                        pltpu.VMEM((dh, MOBA_BLOCK), F32)],
        compiler_params=_params(("arbitrary", "arbitrary", "arbitrary")),
        name="moba_prompt",
    )(slopes, qkv, qkv, qkv)


def _page_sum_kernel(pt_ref, *refs):
    del pt_ref
    *page_refs, o_ref = refs
    s = jnp.sum(page_refs[0][...], axis=0)
    for page_ref in page_refs[1:]:
        s = s + jnp.sum(page_ref[...], axis=0)
    o_ref[...] = s


def _past_block_sums(cache, layer, page_table):
    _, _, page, n_heads, dh = cache.shape
    bs, n_pages = page_table.shape
    ppb = MOBA_BLOCK // page
    nblk = n_pages // ppb

    def page_spec(u):
        return pl.BlockSpec((None, None, page, n_heads, dh),
                            lambda b, r, pt: (layer, pt[b * n_pages + r * ppb + u], 0, 0, 0))

    grid_spec = pltpu.PrefetchScalarGridSpec(
        num_scalar_prefetch=1,
        grid=(bs, nblk),
        in_specs=[page_spec(u) for u in range(ppb)],
        out_specs=pl.BlockSpec((None, None, n_heads, dh), lambda b, r, pt: (b, r, 0, 0)),
    )
    return pl.pallas_call(
        _page_sum_kernel,
        grid_spec=grid_spec,
        out_shape=jax.ShapeDtypeStruct((bs, nblk, n_heads, dh), F32),
        compiler_params=_params(("arbitrary", "arbitrary")),
        name="page_sum",
    )(page_table.reshape(-1), *([cache] * ppb))


def _sample_select_kernel(q_ref, ksum_ref, seg_ref, o_ref, *, n_blocks):
    prod = ksum_ref[...] * (1.0 / MOBA_BLOCK) * q_ref[...]
    gate = jnp.dot(prod, seg_ref[...], precision=lax.Precision.HIGHEST, preferred_element_type=F32)
    rank = _block_ranks(gate, n_blocks, n_blocks)
    bidx = lax.broadcasted_iota(jnp.int32, gate.shape, 0).astype(F32)
    rows = []
    for r in range(MOBA_TOPK):
        rows.append(jnp.sum(jnp.where(rank == float(r), bidx, 0.0), axis=0, keepdims=True))
    rows.append(jnp.zeros((8 - MOBA_TOPK, gate.shape[1]), F32))
    o_ref[...] = jnp.concatenate(rows, axis=0).astype(jnp.int32)


def _sample_select(q_s, ksum, n_heads):
    bs, nblk, _, dh = ksum.shape
    hd = n_heads * dh
    ksum = ksum.reshape(bs, nblk, hd)
    assert nblk >= MOBA_TOPK and nblk % 8 == 0 and n_heads <= LANES
    seg = (jnp.arange(hd)[:, None] // dh == jnp.arange(LANES)[None, :]).astype(F32)
    out = pl.pallas_call(
        functools.partial(_sample_select_kernel, n_blocks=nblk),
        grid=(bs,),
        in_specs=[pl.BlockSpec((None, 1, hd), lambda b: (b, 0, 0)),
                  pl.BlockSpec((None, nblk, hd), lambda b: (b, 0, 0)),
                  pl.BlockSpec((hd, LANES), lambda b: (0, 0))],
        out_specs=pl.BlockSpec((None, 8, LANES), lambda b: (b, 0, 0)),
        out_shape=jax.ShapeDtypeStruct((bs, 8, LANES), jnp.int32),
        compiler_params=_params(("arbitrary",)),
        name="sample_select",
    )(q_s.reshape(bs, 1, hd), ksum, seg)
    return out[:, :MOBA_TOPK, :n_heads].transpose(0, 2, 1)


def _sample_attend_kernel(pt_ref, sel_ref, slopes_ref, q_ref, kn_ref, vn_ref, ck_hbm, cv_hbm, o_ref,
                          kbuf, vbuf, sem, *, layer, n_heads, n_pages, pages_per_block, scale):
    b = pl.program_id(0)
    page, dh = kbuf.shape[2], kbuf.shape[3]
    n_sub = MOBA_TOPK * pages_per_block
    past_len = n_pages * page
    hi = lax.Precision.HIGHEST

    def block_of(h, s):
        return sel_ref[(b * n_heads + h) * MOBA_TOPK + s // pages_per_block]

    def page_copies(h, s):
        pg = pt_ref[b * n_pages + block_of(h, s) * pages_per_block + s % pages_per_block]
        return (pltpu.make_async_copy(ck_hbm.at[layer, pg, :, h, :], kbuf.at[h, s], sem.at[0, h]),
                pltpu.make_async_copy(cv_hbm.at[layer, pg, :, h, :], vbuf.at[h, s], sem.at[1, h]))

    for h in range(n_heads):
        for s in range(n_sub):
            for cp in page_copies(h, s):
                cp.start()

    lane = lax.broadcasted_iota(jnp.int32, (1, n_sub * page), 1)
    for h in range(n_heads):
        for s in range(n_sub):
            for cp in page_copies(h, s):
                cp.wait()
        q = q_ref[h:h + 1, :]
        q8 = jnp.broadcast_to(q, (8, dh))
        kh = kbuf[h].reshape(n_sub * page, dh)
        vh = vbuf[h].reshape(n_sub * page, dh)
        sc = lax.dot_general(q8, kh, (((1,), (1,)), ((), ())), precision=hi,
                             preferred_element_type=F32)[0:1, :] * scale
        pos = jnp.zeros((1, n_sub * page), jnp.int32)
        for s in range(n_sub):
            first = block_of(h, s) * MOBA_BLOCK + (s % pages_per_block) * page
            pos = jnp.where((lane >= s * page) & (lane < (s + 1) * page), first + (lane - s * page), pos)
        sc = sc - slopes_ref[h] * (past_len - pos).astype(F32)
        s_new = jnp.sum(q * kn_ref[h:h + 1, :], axis=-1, keepdims=True) * scale
        m = jnp.maximum(jnp.max(sc, axis=-1, keepdims=True), s_new)
        p = jnp.exp(sc - m)
        p_new = jnp.exp(s_new - m)
        denom = jnp.sum(p, axis=-1, keepdims=True) + p_new
        pv = jnp.dot(jnp.broadcast_to(p, (8, n_sub * page)), vh, precision=hi, preferred_element_type=F32)
        o_ref[h:h + 1, :] = (pv[0:1, :] + p_new * vn_ref[h:h + 1, :]) / denom


def _sample_attend(q_s, k_new, v_new, cache_k, cache_v, layer, page_table, sel, slopes, n_heads):
    bs, hd = q_s.shape
    dh = hd // n_heads
    page = cache_k.shape[2]
    n_pages = page_table.shape[1]
    ppb = MOBA_BLOCK // page
    n_sub = MOBA_TOPK * ppb
    row_spec = pl.BlockSpec((None, n_heads, dh), lambda b, pt, sl: (b, 0, 0))
    grid_spec = pltpu.PrefetchScalarGridSpec(
        num_scalar_prefetch=2,
        grid=(bs,),
        in_specs=[pl.BlockSpec(memory_space=pltpu.SMEM),
                  row_spec, row_spec, row_spec,
                  pl.BlockSpec(memory_space=pl.ANY),
                  pl.BlockSpec(memory_space=pl.ANY)],
        out_specs=pl.BlockSpec((None, n_heads, dh), lambda b, pt, sl: (b, 0, 0)),
        scratch_shapes=[pltpu.VMEM((n_heads, n_sub, page, dh), F32),
                        pltpu.VMEM((n_heads, n_sub, page, dh), F32),
                        pltpu.SemaphoreType.DMA((2, n_heads))],
    )
    out = pl.pallas_call(
        functools.partial(_sample_attend_kernel, layer=layer, n_heads=n_heads, n_pages=n_pages,
                          pages_per_block=ppb, scale=dh ** -0.5),
        grid_spec=grid_spec,
        out_shape=jax.ShapeDtypeStruct((bs, n_heads, dh), F32),
        compiler_params=_params(("arbitrary",)),
        name="sample_attend",
    )(page_table.reshape(-1), sel.reshape(-1), slopes,
      q_s.reshape(bs, n_heads, dh), k_new.reshape(bs, n_heads, dh), v_new.reshape(bs, n_heads, dh),
      cache_k, cache_v)
    return out.reshape(bs, hd)


def _router_kernel(x_ref, w_ref, b_ref, slab_ref, cnt_ref, carry_ref, *, n_tokens):
    i = pl.program_id(0)
    tm = x_ref.shape[0]

    @pl.when(i == 0)
    def _():
        carry_ref[...] = jnp.zeros_like(carry_ref)

    logits = jnp.dot(x_ref[...], w_ref[...], precision=lax.Precision.HIGHEST,
                     preferred_element_type=F32) + b_ref[...]
    lane = lax.broadcasted_iota(jnp.int32, logits.shape, 1)
    lane_f = lane.astype(F32)
    valid = (lax.broadcasted_iota(jnp.int32, (tm, 1), 0) + i * tm) < n_tokens
    vals, onehots = [], []
    work = logits
    for _ in range(TOP_K):
        mx = jnp.max(work, axis=-1, keepdims=True)
        idx = jnp.min(jnp.where(work == mx, lane_f, float(LANES)), axis=-1, keepdims=True)
        oh = lane_f == idx
        vals.append(mx)
        onehots.append(jnp.where(oh & valid, 1.0, 0.0))
        work = jnp.where(oh, -3e38, work)
    exps = [jnp.exp(v - vals[0]) for v in vals]
    denom = exps[0]
    for e in exps[1:]:
        denom = denom + e
    oh_all = onehots[0]
    for oh in onehots[1:]:
        oh_all = oh_all + oh
    r = lax.broadcasted_iota(jnp.int32, (tm, tm), 0)
    c = lax.broadcasted_iota(jnp.int32, (tm, tm), 1)
    tri = jnp.where(c < r, 1.0, 0.0).astype(BF16)
    before = jnp.dot(tri, oh_all.astype(BF16), preferred_element_type=F32) + carry_ref[...]
    slab = jnp.zeros(logits.shape, F32)
    for kk in range(TOP_K):
        idx_k = jnp.sum(onehots[kk] * lane_f, axis=-1, keepdims=True)
        rank_k = jnp.sum(onehots[kk] * before, axis=-1, keepdims=True)
        gate_k = jnp.where(valid, exps[kk] / denom, 0.0)
        slab = jnp.where(lane == kk, idx_k, slab)
        slab = jnp.where(lane == TOP_K + kk, gate_k, slab)
        slab = jnp.where(lane == 2 * TOP_K + kk, rank_k, slab)
    slab_ref[...] = slab
    carry_ref[...] += jnp.sum(oh_all, axis=0, keepdims=True)
    cnt_ref[...] = jnp.broadcast_to(carry_ref[...], cnt_ref.shape)


def _router(x, w, b, n_tokens):
    t, d = x.shape
    n_exp = w.shape[1]
    assert TOP_K <= n_exp <= LANES and 3 * TOP_K <= LANES
    w_pad = jnp.zeros((d, LANES), F32).at[:, :n_exp].set(w)
    b_pad = jnp.full((1, LANES), NEG_BIG, F32).at[0, :n_exp].set(b)
    return pl.pallas_call(
        functools.partial(_router_kernel, n_tokens=n_tokens),
        grid=(t // ROW_TILE,),
        in_specs=[pl.BlockSpec((ROW_TILE, d), lambda i: (i, 0)),
                  pl.BlockSpec((d, LANES), lambda i: (0, 0)),
                  pl.BlockSpec((1, LANES), lambda i: (0, 0))],
        out_specs=[pl.BlockSpec((ROW_TILE, LANES), lambda i: (i, 0)),
                   pl.BlockSpec((8, LANES), lambda i: (0, 0))],
        out_shape=[jax.ShapeDtypeStruct((t, LANES), F32), jax.ShapeDtypeStruct((8, LANES), F32)],
        scratch_shapes=[pltpu.VMEM((1, LANES), F32)],
        compiler_params=_params(("arbitrary",)),
        name="router",
    )(x, w_pad, b_pad)


def _expert_kernel(we_ref, ws_ref, wn_ref, pair_ref, x_hbm, wg_ref, wu_ref, wd_ref, bg_ref, bu_ref, bd_ref,
                   y_hbm, xbuf, xb_ref, acc_ref, gsem, ssem, *, n_tok):
    del we_ref
    w = pl.program_id(0)
    c = pl.program_id(1)
    n_items = pl.num_programs(0)
    n_chunks = pl.num_programs(1)
    n_rows = wn_ref[w]
    slot = w % 2
    rows_per_step = -(-SUPER_TILE // (n_chunks - 1))

    def gather_copy(r, pair):
        return pltpu.make_async_copy(x_hbm.at[pl.ds(pair // TOP_K, 1), :], xbuf.at[pl.ds(r, 1), :], gsem)

    def scatter_copy(sl, r, pair):
        dst = (pair % TOP_K) * n_tok + pair // TOP_K
        return pltpu.make_async_copy(acc_ref.at[sl, pl.ds(r, 1), :], y_hbm.at[pl.ds(dst, 1), :], ssem.at[sl])

    def start_gathers(item, lo, hi):
        base = ws_ref[item]

        def body(r, carry):
            gather_copy(r, pair_ref[base + r]).start()
            return carry

        lax.fori_loop(lo, hi, body, 0)

    def wait_gathers(count):
        def body(r, carry):
            gather_copy(r, 0).wait()
            return carry

        lax.fori_loop(0, count, body, 0)

    def wait_scatters(sl, count):
        def body(r, carry):
            scatter_copy(sl, r, 0).wait()
            return carry

        lax.fori_loop(0, count, body, 0)

    @pl.when(c == 0)
    def _():
        @pl.when(w == 0)
        def _():
            xbuf[...] = jnp.zeros_like(xbuf)
            start_gathers(0, 0, n_rows)

        @pl.when(w >= 2)
        def _():
            wait_scatters(slot, wn_ref[jnp.maximum(w - 2, 0)])

        wait_gathers(n_rows)
        xb_ref[...] = xbuf[...].astype(BF16)
        acc_ref[slot] = jnp.broadcast_to(bd_ref[...], acc_ref.shape[1:])

    @pl.when((c > 0) & (w + 1 < n_items))
    def _():
        nxt = jnp.minimum(w + 1, n_items - 1)
        n_next = wn_ref[nxt]
        start_gathers(nxt, jnp.minimum((c - 1) * rows_per_step, n_next), jnp.minimum(c * rows_per_step, n_next))

    n_sub = (n_rows + SUB_TILE - 1) // SUB_TILE
    half = acc_ref.shape[2] // 2
    for k in range(1, SUPER_TILE // SUB_TILE + 1):
        rows = slice(0, k * SUB_TILE)

        @pl.when(n_sub == k)
        def _():
            xb = xb_ref[rows, :]
            gate = jnp.dot(xb, wg_ref[...].astype(BF16), preferred_element_type=F32) + bg_ref[...]
            up = jnp.dot(xb, wu_ref[...].astype(BF16), preferred_element_type=F32) + bu_ref[...]
            gate = jnp.minimum(gate, SWIGLU_LIMIT)
            up = jnp.clip(up, -SWIGLU_LIMIT, SWIGLU_LIMIT)
            act = (gate * (1.0 / (1.0 + jnp.exp(-SWIGLU_ALPHA * gate))) * (up + 1.0)).astype(BF16)
            for cols in (slice(0, half), slice(half, 2 * half)):
                acc_ref[slot, rows, cols] += jnp.dot(act, wd_ref[:, cols].astype(BF16), preferred_element_type=F32)

    @pl.when(c == n_chunks - 1)
    def _():
        base = ws_ref[w]

        def body(r, carry):
            scatter_copy(slot, r, pair_ref[base + r]).start()
            return carry

        lax.fori_loop(0, n_rows, body, 0)

        @pl.when(w == n_items - 1)
        def _():
            wait_scatters(1 - slot, wn_ref[jnp.maximum(w - 1, 0)])
            wait_scatters(slot, n_rows)


def _experts(x, item_expert, item_start, item_rows, pairs, w_gate_up, b_gate_up, w_down, b_down, layer):
    t, d = x.shape
    n_exp, _, two_de = w_gate_up.shape[1:]
    de = two_de // 2
    tn = _pick_tile(de, 256, LANES)
    nc = de // tn
    assert nc >= 2
    n_items = item_expert.shape[0]
    assert n_items >= 2
    bgu = b_gate_up.reshape(b_gate_up.shape[0], n_exp, 1, two_de)
    bdn = b_down.reshape(b_down.shape[0], n_exp, 1, d)

    def chunk(i, c, wn):
        return jnp.where(wn[i] > 0, c, nc - 1)

    grid_spec = pltpu.PrefetchScalarGridSpec(
        num_scalar_prefetch=4,
        grid=(n_items, nc),
        in_specs=[pl.BlockSpec(memory_space=pl.ANY),
                  pl.BlockSpec((None, None, d, tn), lambda i, c, we, ws, wn, pr: (layer, we[i], 0, chunk(i, c, wn))),
                  pl.BlockSpec((None, None, d, tn), lambda i, c, we, ws, wn, pr: (layer, we[i], 0, nc + chunk(i, c, wn))),
                  pl.BlockSpec((None, None, tn, d), lambda i, c, we, ws, wn, pr: (layer, we[i], chunk(i, c, wn), 0)),
                  pl.BlockSpec((None, None, 1, tn), lambda i, c, we, ws, wn, pr: (layer, we[i], 0, chunk(i, c, wn))),
                  pl.BlockSpec((None, None, 1, tn), lambda i, c, we, ws, wn, pr: (layer, we[i], 0, nc + chunk(i, c, wn))),
                  pl.BlockSpec((None, None, 1, d), lambda i, c, we, ws, wn, pr: (layer, we[i], 0, 0))],
        out_specs=pl.BlockSpec(memory_space=pl.ANY),
        scratch_shapes=[pltpu.VMEM((SUPER_TILE, d), F32),
                        pltpu.VMEM((SUPER_TILE, d), BF16),
                        pltpu.VMEM((2, SUPER_TILE, d), F32),
                        pltpu.SemaphoreType.DMA(()),
                        pltpu.SemaphoreType.DMA((2,))],
    )
    return pl.pallas_call(
        functools.partial(_expert_kernel, n_tok=t),
        grid_spec=grid_spec,
        out_shape=jax.ShapeDtypeStruct((TOP_K * t, d), F32),
        compiler_params=_params(("arbitrary", "arbitrary")),
        name="experts",
    )(item_expert, item_start, item_rows, pairs, x, w_gate_up, w_gate_up, w_down, bgu, bgu, bdn)


def _sample_expert_kernel(el_ref, nu_ref, x_ref, comb_ref, wg_ref, wu_ref, wd_ref, bg_ref, bu_ref, bd_ref,
                          g_ref, b_ref, o_ref, acc_ref, *, alpha):
    del el_ref
    j = pl.program_id(0)
    c = pl.program_id(1)

    @pl.when((j == 0) & (c == 0))
    def _():
        acc_ref[...] = jnp.zeros_like(acc_ref)

    @pl.when(j < nu_ref[0])
    def _():
        x = x_ref[...]
        gate = jnp.minimum(_dot3(x, wg_ref[...]) + bg_ref[...], SWIGLU_LIMIT)
        up = jnp.clip(_dot3(x, wu_ref[...]) + bu_ref[...], -SWIGLU_LIMIT, SWIGLU_LIMIT)
        act = gate * (1.0 / (1.0 + jnp.exp(-SWIGLU_ALPHA * gate))) * (up + 1.0)
        part = _dot3(act, wd_ref[...])
        part = part + jnp.where(c == 0, 1.0, 0.0) * bd_ref[...]
        acc_ref[...] += comb_ref[:, 0:1] * part

    @pl.when((j == pl.num_programs(0) - 1) & (c == pl.num_programs(1) - 1))
    def _():
        o_ref[...] = _layer_norm(alpha * x_ref[...] + acc_ref[...], g_ref[...], b_ref[...])


def _sample_moe_layer(x, layer, router_w, router_b, w_gate_up, b_gate_up, w_down, b_down, ln_g, ln_b, alpha):
    m, d = x.shape
    n_exp = router_w.shape[1]
    two_de = w_gate_up.shape[3]
    de = two_de // 2
    x_pad = jnp.zeros((ROW_TILE, d), F32).at[:m].set(x)
    slab, _ = _router(x_pad, router_w, router_b, m)
    idx = slab[:m, :TOP_K].astype(jnp.int32)
    gates = slab[:m, TOP_K:2 * TOP_K]
    comb = jnp.einsum("tk,tke->te", gates, jax.nn.one_hot(idx, n_exp, dtype=F32))
    used = jnp.any(comb > 0.0, axis=0)
    n_used = jnp.sum(used).astype(jnp.int32)
    order = jnp.argsort(jnp.logical_not(used), stable=True).astype(jnp.int32)
    n_slots = min(n_exp, m * TOP_K)
    slots = jnp.minimum(jnp.arange(n_slots, dtype=jnp.int32), n_used - 1)
    e_list = order[slots]
    comb_b = jnp.broadcast_to(comb.T[:, :, None], (n_exp, m, LANES))
    tn = _pick_tile(de, 256, LANES)
    nc = de // tn
    bgu = b_gate_up.reshape(b_gate_up.shape[0], n_exp, 1, two_de)
    bdn = b_down.reshape(b_down.shape[0], n_exp, 1, d)

    def chunk(j, c, nu):
        return jnp.where(j < nu[0], c, nc - 1)

    row = pl.BlockSpec((1, d), lambda j, c, el, nu: (0, 0))
    grid_spec = pltpu.PrefetchScalarGridSpec(
        num_scalar_prefetch=2,
        grid=(n_slots, nc),
        in_specs=[pl.BlockSpec((m, d), lambda j, c, el, nu: (0, 0)),
                  pl.BlockSpec((None, m, LANES), lambda j, c, el, nu: (el[j], 0, 0)),
                  pl.BlockSpec((None, None, d, tn), lambda j, c, el, nu: (layer, el[j], 0, chunk(j, c, nu))),
                  pl.BlockSpec((None, None, d, tn), lambda j, c, el, nu: (layer, el[j], 0, nc + chunk(j, c, nu))),
                  pl.BlockSpec((None, None, tn, d), lambda j, c, el, nu: (layer, el[j], chunk(j, c, nu), 0)),
                  pl.BlockSpec((None, None, 1, tn), lambda j, c, el, nu: (layer, el[j], 0, chunk(j, c, nu))),
                  pl.BlockSpec((None, None, 1, tn), lambda j, c, el, nu: (layer, el[j], 0, nc + chunk(j, c, nu))),
                  pl.BlockSpec((None, None, 1, d), lambda j, c, el, nu: (layer, el[j], 0, 0)),
                  row, row],
        out_specs=pl.BlockSpec((m, d), lambda j, c, el, nu: (0, 0)),
        scratch_shapes=[pltpu.VMEM((m, d), F32)],
    )
    return pl.pallas_call(
        functools.partial(_sample_expert_kernel, alpha=alpha),
        grid_spec=grid_spec,
        out_shape=jax.ShapeDtypeStruct((m, d), F32),
        compiler_params=_params(("arbitrary", "arbitrary")),
        name="sample_experts",
    )(e_list, n_used.reshape(1), x, comb_b, w_gate_up, w_gate_up, w_down, bgu, bgu, bdn,
      ln_g.reshape(1, d), ln_b.reshape(1, d))


def _combine_ln_kernel(*refs, alpha):
    y_refs = refs[:TOP_K]
    slab_ref, x_ref, g_ref, b_ref, o_ref = refs[TOP_K:]
    f = alpha * x_ref[...]
    for kk in range(TOP_K):
        f = f + slab_ref[:, TOP_K + kk:TOP_K + kk + 1] * y_refs[kk][...]
    o_ref[...] = _layer_norm(f, g_ref[...], b_ref[...])


def _combine_ln(y4, slab, x, g, b, alpha):
    t, d = x.shape
    nt = t // ROW_TILE

    def y_spec(kk):
        return pl.BlockSpec((ROW_TILE, d), lambda i: (kk * nt + i, 0))

    return pl.pallas_call(
        functools.partial(_combine_ln_kernel, alpha=alpha),
        grid=(nt,),
        in_specs=[y_spec(kk) for kk in range(TOP_K)] + [
            pl.BlockSpec((ROW_TILE, LANES), lambda i: (i, 0)),
            pl.BlockSpec((ROW_TILE, d), lambda i: (i, 0)),
            pl.BlockSpec((1, d), lambda i: (0, 0)),
            pl.BlockSpec((1, d), lambda i: (0, 0))],
        out_specs=pl.BlockSpec((ROW_TILE, d), lambda i: (i, 0)),
        out_shape=jax.ShapeDtypeStruct((t, d), F32),
        compiler_params=_params(("arbitrary",)),
        name="combine_ln",
    )(*([y4] * TOP_K), slab, x, g.reshape(1, d), b.reshape(1, d))


def _moe_layer(x, layer, router_w, router_b, w_gate_up, b_gate_up, w_down, b_down, ln_g, ln_b, alpha):
    t, d = x.shape
    n_exp = router_w.shape[1]
    slab, cnt = _router(x, router_w, router_b, t)
    idx = slab[:, :TOP_K].astype(jnp.int32)
    rank = slab[:, 2 * TOP_K:3 * TOP_K].astype(jnp.int32)
    counts = cnt[0, :n_exp].astype(jnp.int32)
    first_row = jnp.cumsum(counts) - counts
    pos = first_row[idx] + rank
    pair_id = jnp.arange(t * TOP_K, dtype=jnp.int32).reshape(t, TOP_K)
    pairs = jnp.zeros((t * TOP_K,), jnp.int32).at[pos.reshape(-1)].set(pair_id.reshape(-1))
    n_items = (t * TOP_K) // SUPER_TILE + n_exp
    items_per_exp = (counts + SUPER_TILE - 1) // SUPER_TILE
    item_end = jnp.cumsum(items_per_exp)
    ids = jnp.arange(n_items, dtype=jnp.int32)
    live = ids < item_end[-1]
    item_expert = jnp.minimum(jnp.searchsorted(item_end, jnp.minimum(ids, item_end[-1] - 1), side="right"),
                              n_exp - 1).astype(jnp.int32)
    local = ids - (item_end - items_per_exp)[item_expert]
    item_start = jnp.where(live, first_row[item_expert] + local * SUPER_TILE, 0).astype(jnp.int32)
    item_rows = jnp.where(live, jnp.clip(counts[item_expert] - local * SUPER_TILE, 0, SUPER_TILE), 0).astype(jnp.int32)
    y4 = _experts(x, item_expert, item_start, item_rows, pairs, w_gate_up, b_gate_up, w_down, b_down, layer)
    return _combine_ln(y4, slab, x, ln_g, ln_b, alpha)


def _gmlp_prompt(x, w_in, b_in, lnv_g, lnv_b, w_s, b_s, w_out, ln_g, ln_b, alpha):
    z = _mm_bias(x, w_in, b_in, gelu=True)
    causal = jnp.tril(jnp.ones((CHUNK, CHUNK), dtype=bool))
    vn, gated = _spatial_mix(z, jnp.where(causal[None], w_s, 0.0), b_s.T, lnv_g, lnv_b)
    return _mm_resid_ln(gated, w_out, x, ln_g, ln_b, alpha), vn


def _gmlp_sample(x, w_in, b_in, lnv_g, lnv_b, w_s, b_s, w_out, ln_g, ln_b, alpha):
    z = _mm_bias(x, w_in, b_in, gelu=True, hi=True)
    group_dim = lnv_g.shape[0] // w_s.shape[0]
    vn, gated = _sample_gate(z, jnp.repeat(w_s[:, 0, 0], group_dim), jnp.repeat(b_s[:, 0], group_dim), lnv_g, lnv_b)
    return _mm_resid_ln(gated, w_out, x, ln_g, ln_b, alpha, hi=True), vn


def kernel(x_prompt, x_sample, cache_k, cache_v, page_table, w_in_a, b_in_a, ln_v_g, ln_v_b, w_s, b_s, w_out_a, w_qkv, w_o, ln_mix_g, ln_mix_b, ln_ffn_g, ln_ffn_b, router_w, router_b, w_gate_up, b_gate_up, w_down, b_down):
    bp, sp, d = x_prompt.shape
    bs, ss, _ = x_sample.shape
    depth = ln_mix_g.shape[0]
    n_heads = cache_k.shape[3]
    dh = cache_k.shape[4]
    page = cache_k.shape[2]
    n_pages = page_table.shape[1]
    assert ss == 1 and sp % CHUNK == 0 and sp % MOBA_BLOCK == 0
    assert MOBA_BLOCK % page == 0 and (n_pages * page) % MOBA_BLOCK == 0
    alpha = (2 * depth) ** 0.25
    n_prompt = bp * sp
    n_tokens = n_prompt + bs
    slopes = 2.0 ** (-8.0 * jnp.arange(1, n_heads + 1, dtype=F32) / n_heads)
    no_bias = jnp.zeros((3 * d,), F32)

    xp = x_prompt.reshape(n_prompt, d)
    xs = x_sample.reshape(bs, d)
    v_rows_p, v_rows_s, k_p, v_p, k_s, v_s = [], [], [], [], [], []
    for i in range(depth):
        li = i // 2
        g_mix, b_mix = ln_mix_g[i], ln_mix_b[i]
        if i % 2 == 0:
            mixer = (w_in_a[li], b_in_a[li], ln_v_g[li], ln_v_b[li], w_s[li], b_s[li], w_out_a[li], g_mix, b_mix, alpha)
            xp, vn_p = _gmlp_prompt(xp, *mixer)
            xs, vn_s = _gmlp_sample(xs, *mixer)
            last0 = ((sp - 1) // CHUNK) * CHUNK
            v_rows_p.append(vn_p.reshape(bp, sp, -1)[:, last0:])
            v_rows_s.append(vn_s.reshape(bs, ss, -1))
        else:
            qkv = _mm_bias(xp, w_qkv[li], no_bias, gelu=False)
            attn_p = _moba_prompt(qkv, slopes, bp, sp, n_heads)
            xp = _mm_resid_ln(attn_p, w_o[li], xp, g_mix, b_mix, alpha)
            k_p.append(qkv[:, d:2 * d].reshape(bp, sp, n_heads, dh))
            v_p.append(qkv[:, 2 * d:].reshape(bp, sp, n_heads, dh))

            qkv_s = _mm_bias(xs, w_qkv[li], no_bias, gelu=False, hi=True)
            q_s, k_new, v_new = qkv_s[:, :d], qkv_s[:, d:2 * d], qkv_s[:, 2 * d:]
            ksum = _past_block_sums(cache_k, li, page_table)
            sel = _sample_select(q_s, ksum, n_heads)
            attn_s = _sample_attend(q_s, k_new, v_new, cache_k, cache_v, li, page_table, sel, slopes, n_heads)
            xs = _mm_resid_ln(attn_s, w_o[li], xs, g_mix, b_mix, alpha, hi=True)
            k_s.append(k_new.reshape(bs, ss, n_heads, dh))
            v_s.append(v_new.reshape(bs, ss, n_heads, dh))
        moe = (router_w[i], router_b[i], w_gate_up, b_gate_up, w_down, b_down, ln_ffn_g[i], ln_ffn_b[i], alpha)
        if i < depth - 1:
            xp = _moe_layer(xp, i, *moe)
            xs = _sample_moe_layer(xs, i, *moe)
        else:
            t = _round_up(n_tokens, ROW_TILE)
            x = _moe_layer(jnp.concatenate([xp, xs, jnp.zeros((t - n_tokens, d), F32)], axis=0), i, *moe)
            xp, xs = x[:n_prompt], x[n_prompt:n_tokens]
    return (xp.reshape(bp, sp, d), xs.reshape(bs, ss, d), jnp.stack(v_rows_p), jnp.stack(v_rows_s),
            jnp.stack(k_p), jnp.stack(v_p), jnp.stack(k_s), jnp.stack(v_s))
```

```python
import functools

import jax
import jax.numpy as jnp
from jax import lax
from jax.experimental import pallas as pl
from jax.experimental.pallas import tpu as pltpu

CHUNK = 128
MOBA_BLOCK = 256
MOBA_TOPK = 3
TOP_K = 4
SWIGLU_LIMIT = 7.0
SWIGLU_ALPHA = 1.702
LN_EPS = 1e-5
NEG_BIG = -1e30
LANES = 128
ROW_TILE = 256
SUPER_TILE = 1024
SUB_TILE = 256
VMEM_LIMIT = 56 * 1024 * 1024

F32 = jnp.float32
BF16 = jnp.bfloat16


def _round_up(n, m):
    return -(-n // m) * m


def _pick_tile(n, limit, mult):
    best = None
    for t in range(mult, min(n, limit) + 1, mult):
        if n % t == 0:
            best = t
    assert best is not None, (n, limit, mult)
    return best


def _params(sem):
    return pltpu.CompilerParams(dimension_semantics=sem, vmem_limit_bytes=VMEM_LIMIT)


def _layer_norm(y, g, b):
    mu = jnp.mean(y, axis=-1, keepdims=True)
    d = y - mu
    var = jnp.mean(d * d, axis=-1, keepdims=True)
    return d * lax.rsqrt(var + LN_EPS) * g + b


def _split_bf16(x):
    hi = x.astype(BF16)
    return hi, (x - hi.astype(F32)).astype(BF16)


def _dot3(x, w):
    xh, xl = _split_bf16(x)
    wh, wl = _split_bf16(w)
    m = x.shape[0]
    r = jnp.dot(jnp.concatenate([xh, xl], axis=0), wh, preferred_element_type=F32)
    return r[:m] + r[m:] + jnp.dot(xh, wl, preferred_element_type=F32)


def _mm_gelu_kernel(x_ref, w_ref, b_ref, o_ref, *scratch, hi, gelu):
    if hi:
        h = _dot3(x_ref[...], w_ref[...])
    else:
        xb_ref, = scratch

        @pl.when(pl.program_id(1) == 0)
        def _():
            xb_ref[...] = x_ref[...].astype(BF16)

        h = jnp.dot(xb_ref[...], w_ref[...].astype(BF16), preferred_element_type=F32)
    h = h + b_ref[...]
    o_ref[...] = 0.5 * h * (1.0 + lax.erf(h * (2.0 ** -0.5))) if gelu else h


def _mm_bias(x, w, b, *, gelu, hi=False):
    m, k = x.shape
    n = w.shape[1]
    tm = _pick_tile(m, 1024, 8)
    tn = _pick_tile(n, 512, LANES)
    return pl.pallas_call(
        functools.partial(_mm_gelu_kernel, hi=hi, gelu=gelu),
        grid=(m // tm, n // tn),
        in_specs=[pl.BlockSpec((tm, k), lambda i, j: (i, 0)),
                  pl.BlockSpec((k, tn), lambda i, j: (0, j)),
                  pl.BlockSpec((1, tn), lambda i, j: (0, j))],
        out_specs=pl.BlockSpec((tm, tn), lambda i, j: (i, j)),
        out_shape=jax.ShapeDtypeStruct((m, n), F32),
        scratch_shapes=[] if hi else [pltpu.VMEM((tm, k), BF16)],
        compiler_params=_params(("arbitrary", "arbitrary")),
        name="mm_bias_hi" if hi else "mm_bias",
    )(x, w, b.reshape(1, n))


def _mm_resid_ln_kernel(a_ref, w_ref, x_ref, g_ref, b_ref, o_ref, acc_ref, *, alpha, hi):
    kk = pl.program_id(1)
    if hi:
        part = _dot3(a_ref[...], w_ref[...])
    else:
        part = jnp.dot(a_ref[...].astype(BF16), w_ref[...].astype(BF16), preferred_element_type=F32)

    @pl.when(kk == 0)
    def _():
        acc_ref[...] = part

    @pl.when(kk > 0)
    def _():
        acc_ref[...] += part

    @pl.when(kk == pl.num_programs(1) - 1)
    def _():
        o_ref[...] = _layer_norm(alpha * x_ref[...] + acc_ref[...], g_ref[...], b_ref[...])


def _mm_resid_ln(a, w, x, g, b, alpha, hi=False):
    m, k = a.shape
    n = w.shape[1]
    tm = _pick_tile(m, 512, 8)
    tk = _pick_tile(k, 512, LANES)
    return pl.pallas_call(
        functools.partial(_mm_resid_ln_kernel, alpha=alpha, hi=hi),
        grid=(m // tm, k // tk),
        in_specs=[pl.BlockSpec((tm, tk), lambda i, j: (i, j)),
                  pl.BlockSpec((tk, n), lambda i, j: (j, 0)),
                  pl.BlockSpec((tm, n), lambda i, j: (i, 0)),
                  pl.BlockSpec((1, n), lambda i, j: (0, 0)),
                  pl.BlockSpec((1, n), lambda i, j: (0, 0))],
        out_specs=pl.BlockSpec((tm, n), lambda i, j: (i, 0)),
        out_shape=jax.ShapeDtypeStruct((m, n), F32),
        scratch_shapes=[pltpu.VMEM((tm, n), F32)],
        compiler_params=_params(("arbitrary", "arbitrary")),
        name="mm_resid_ln_hi" if hi else "mm_resid_ln",
    )(a, w, x, g.reshape(1, n), b.reshape(1, n))


def _spatial_mix_kernel(u_ref, v_ref, ws_ref, bs_ref, g_ref, b_ref, vn_ref, o_ref, *, n_groups):
    vn = _layer_norm(v_ref[...], g_ref[...], b_ref[...])
    vn_ref[...] = vn
    vb = vn.astype(BF16)
    gd = vn.shape[1] // n_groups
    for gi in range(n_groups):
        sl = slice(gi * gd, (gi + 1) * gd)
        mixed = jnp.dot(ws_ref[gi].astype(BF16), vb[:, sl], preferred_element_type=F32)
        mixed = mixed + bs_ref[:, gi:gi + 1]
        o_ref[:, sl] = (u_ref[:, sl] * mixed).astype(BF16)


def _spatial_mix(z, ws_causal, bs_t, ln_g, ln_b):
    t, two_da = z.shape
    da = two_da // 2
    n_groups = ws_causal.shape[0]
    return pl.pallas_call(
        functools.partial(_spatial_mix_kernel, n_groups=n_groups),
        grid=(t // CHUNK,),
        in_specs=[pl.BlockSpec((CHUNK, da), lambda i: (i, 0)),
                  pl.BlockSpec((CHUNK, da), lambda i: (i, 1)),
                  pl.BlockSpec((n_groups, CHUNK, CHUNK), lambda i: (0, 0, 0)),
                  pl.BlockSpec((CHUNK, n_groups), lambda i: (0, 0)),
                  pl.BlockSpec((1, da), lambda i: (0, 0)),
                  pl.BlockSpec((1, da), lambda i: (0, 0))],
        out_specs=[pl.BlockSpec((CHUNK, da), lambda i: (i, 0)),
                   pl.BlockSpec((CHUNK, da), lambda i: (i, 0))],
        out_shape=[jax.ShapeDtypeStruct((t, da), F32), jax.ShapeDtypeStruct((t, da), BF16)],
        compiler_params=_params(("arbitrary",)),
        name="spatial_mix",
    )(z, z, ws_causal, bs_t, ln_g.reshape(1, da), ln_b.reshape(1, da))


def _sample_gate_kernel(u_ref, v_ref, w_ref, c_ref, g_ref, b_ref, vn_ref, o_ref):
    vn = _layer_norm(v_ref[...], g_ref[...], b_ref[...])
    vn_ref[...] = vn
    o_ref[...] = u_ref[...] * (vn * w_ref[...] + c_ref[...])


def _sample_gate(z, w00, b0, ln_g, ln_b):
    m, two_da = z.shape
    da = two_da // 2
    row = pl.BlockSpec((1, da), lambda i: (0, 0))
    return pl.pallas_call(
        _sample_gate_kernel,
        grid=(1,),
        in_specs=[pl.BlockSpec((m, da), lambda i: (0, 0)), pl.BlockSpec((m, da), lambda i: (0, 1)),
                  row, row, row, row],
        out_specs=[pl.BlockSpec((m, da), lambda i: (0, 0)), pl.BlockSpec((m, da), lambda i: (0, 0))],
        out_shape=[jax.ShapeDtypeStruct((m, da), F32), jax.ShapeDtypeStruct((m, da), F32)],
        compiler_params=_params(("arbitrary",)),
        name="sample_gate",
    )(z, z, w00.reshape(1, da), b0.reshape(1, da), ln_g.reshape(1, da), ln_b.reshape(1, da))


def _block_ranks(g, n_valid, n_blocks):
    blk = lax.broadcasted_iota(jnp.int32, g.shape, 0)
    past = blk < n_valid
    rows = []
    for n in range(n_blocks):
        gn = g[n:n + 1, :]
        beats = jnp.where(past & ((g > gn) | ((g == gn) & (blk < n))), 1.0, 0.0)
        rows.append(jnp.sum(beats, axis=0, keepdims=True))
    return jnp.concatenate(rows, axis=0)


def _moba_prompt_kernel(slopes_ref, q_ref, k_ref, v_ref, o_ref,
                        kb_ref, vt_ref, kmean_ref, bias_ref, sel_ref, acc_ref, *, n_blocks, scale):
    h = pl.program_id(1)
    qi = pl.program_id(2)
    slope = slopes_ref[h]
    blk = MOBA_BLOCK
    dh = q_ref.shape[1]
    krow = lax.broadcasted_iota(jnp.int32, (blk, blk), 0)
    qcol = lax.broadcasted_iota(jnp.int32, (blk, blk), 1)

    @pl.when(qi == 0)
    def _():
        kmean_ref[...] = jnp.zeros_like(kmean_ref)
        for n in range(n_blocks):
            kn = k_ref[n * blk:(n + 1) * blk, :]
            kb_ref[n] = kn.astype(BF16)
            vt_ref[n] = v_ref[n * blk:(n + 1) * blk, :].T.astype(BF16)
            kmean_ref[n:n + 1, :] = jnp.mean(kn, axis=0, keepdims=True)
        bias_ref[...] = slope * (qcol - krow).astype(F32)

    q_t = q_ref[...].T
    gate = jnp.dot(kmean_ref[...], q_t, precision=lax.Precision.HIGHEST, preferred_element_type=F32)
    rank = _block_ranks(gate, qi, n_blocks)
    bidx = lax.broadcasted_iota(jnp.int32, gate.shape, 0)
    chosen = (rank[0:n_blocks] < MOBA_TOPK) & (bidx[0:n_blocks] < qi)
    sel_ref[0:n_blocks, :] = (jnp.where(chosen, 0.0, NEG_BIG)
                              - slope * ((qi - bidx[0:n_blocks]) * blk).astype(F32))
    qb = (q_t * scale).astype(BF16)

    s = jnp.dot(kb_ref[qi], qb, preferred_element_type=F32) - bias_ref[...]
    s = jnp.where(krow <= qcol, s, NEG_BIG)
    m0 = jnp.max(s, axis=0, keepdims=True)
    p = jnp.exp(s - m0)
    l0 = jnp.sum(p, axis=0, keepdims=True)
    acc_ref[...] = jnp.dot(vt_ref[qi], p.astype(BF16), preferred_element_type=F32)

    def masked_scores(j):
        return jnp.dot(kb_ref[j], qb, preferred_element_type=F32) - bias_ref[...] + sel_ref[pl.ds(j, 1), :]

    def body(i, carry):
        m_old, l_old = carry
        ja, jb = 2 * i, 2 * i + 1
        sa, sb = masked_scores(ja), masked_scores(jb)
        m_new = jnp.maximum(m_old, jnp.maximum(jnp.max(sa, axis=0, keepdims=True),
                                               jnp.max(sb, axis=0, keepdims=True)))
        a = jnp.exp(m_old - m_new)
        pa = jnp.exp(sa - m_new)
        pb = jnp.exp(sb - m_new)
        acc_ref[...] = (a * acc_ref[...] + jnp.dot(vt_ref[ja], pa.astype(BF16), preferred_element_type=F32)
                        + jnp.dot(vt_ref[jb], pb.astype(BF16), preferred_element_type=F32))
        return m_new, a * l_old + jnp.sum(pa, axis=0, keepdims=True) + jnp.sum(pb, axis=0, keepdims=True)

    _, l_fin = lax.fori_loop(0, (qi + 1) // 2, body, (m0, l0))
    o_ref[...] = (acc_ref[...] / l_fin).T


def _moba_prompt(qkv, slopes, bp, sp, n_heads):
    d = qkv.shape[1] // 3
    dh = d // n_heads
    assert sp % MOBA_BLOCK == 0 and dh % LANES == 0
    nq = sp // MOBA_BLOCK
    nq_pad = _round_up(nq, 8)
    return pl.pallas_call(
        functools.partial(_moba_prompt_kernel, n_blocks=nq, scale=dh ** -0.5),
        grid=(bp, n_heads, nq),
        in_specs=[pl.BlockSpec(memory_space=pltpu.SMEM),
                  pl.BlockSpec((MOBA_BLOCK, dh), lambda b, h, i: (b * nq + i, h)),
                  pl.BlockSpec((sp, dh), lambda b, h, i: (b, n_heads + h)),
                  pl.BlockSpec((sp, dh), lambda b, h, i: (b, 2 * n_heads + h))],
        out_specs=pl.BlockSpec((MOBA_BLOCK, dh), lambda b, h, i: (b * nq + i, h)),
        out_shape=jax.ShapeDtypeStruct((bp * sp, d), F32),
        scratch_shapes=[pltpu.VMEM((nq, MOBA_BLOCK, dh), BF16),
                        pltpu.VMEM((nq, dh, MOBA_BLOCK), BF16),
                        pltpu.VMEM((nq_pad, dh), F32),
                        pltpu.VMEM((MOBA_BLOCK, MOBA_BLOCK), F32),
                        pltpu.VMEM((nq_pad, MOBA_BLOCK), F32),
                        pltpu.VMEM((dh, MOBA_BLOCK), F32)],
        compiler_params=_params(("arbitrary", "arbitrary", "arbitrary")),
        name="moba_prompt",
    )(slopes, qkv, qkv, qkv)


def _page_sum_kernel(pt_ref, *refs):
    del pt_ref
    *page_refs, o_ref = refs
    s = jnp.sum(page_refs[0][...], axis=0)
    for page_ref in page_refs[1:]:
        s = s + jnp.sum(page_ref[...], axis=0)
    o_ref[...] = s


def _past_block_sums(cache, layer, page_table):
    _, _, page, n_heads, dh = cache.shape
    bs, n_pages = page_table.shape
    ppb = MOBA_BLOCK // page
    nblk = n_pages // ppb

    def page_spec(u):
        return pl.BlockSpec((None, None, page, n_heads, dh),
                            lambda b, r, pt: (layer, pt[b * n_pages + r * ppb + u], 0, 0, 0))

    grid_spec = pltpu.PrefetchScalarGridSpec(
        num_scalar_prefetch=1,
        grid=(bs, nblk),
        in_specs=[page_spec(u) for u in range(ppb)],
        out_specs=pl.BlockSpec((None, None, n_heads, dh), lambda b, r, pt: (b, r, 0, 0)),
    )
    return pl.pallas_call(
        _page_sum_kernel,
        grid_spec=grid_spec,
        out_shape=jax.ShapeDtypeStruct((bs, nblk, n_heads, dh), F32),
        compiler_params=_params(("arbitrary", "arbitrary")),
        name="page_sum",
    )(page_table.reshape(-1), *([cache] * ppb))


def _sample_select_kernel(q_ref, ksum_ref, seg_ref, o_ref, *, n_blocks):
    prod = ksum_ref[...] * (1.0 / MOBA_BLOCK) * q_ref[...]
    gate = jnp.dot(prod, seg_ref[...], precision=lax.Precision.HIGHEST, preferred_element_type=F32)
    rank = _block_ranks(gate, n_blocks, n_blocks)
    bidx = lax.broadcasted_iota(jnp.int32, gate.shape, 0).astype(F32)
    rows = []
    for r in range(MOBA_TOPK):
        rows.append(jnp.sum(jnp.where(rank == float(r), bidx, 0.0), axis=0, keepdims=True))
    rows.append(jnp.zeros((8 - MOBA_TOPK, gate.shape[1]), F32))
    o_ref[...] = jnp.concatenate(rows, axis=0).astype(jnp.int32)


def _sample_select(q_s, ksum, n_heads):
    bs, nblk, _, dh = ksum.shape
    hd = n_heads * dh
    ksum = ksum.reshape(bs, nblk, hd)
    assert nblk >= MOBA_TOPK and nblk % 8 == 0 and n_heads <= LANES
    seg = (jnp.arange(hd)[:, None] // dh == jnp.arange(LANES)[None, :]).astype(F32)
    out = pl.pallas_call(
        functools.partial(_sample_select_kernel, n_blocks=nblk),
        grid=(bs,),
        in_specs=[pl.BlockSpec((None, 1, hd), lambda b: (b, 0, 0)),
                  pl.BlockSpec((None, nblk, hd), lambda b: (b, 0, 0)),
                  pl.BlockSpec((hd, LANES), lambda b: (0, 0))],
        out_specs=pl.BlockSpec((None, 8, LANES), lambda b: (b, 0, 0)),
        out_shape=jax.ShapeDtypeStruct((bs, 8, LANES), jnp.int32),
        compiler_params=_params(("arbitrary",)),
        name="sample_select",
    )(q_s.reshape(bs, 1, hd), ksum, seg)
    return out[:, :MOBA_TOPK, :n_heads].transpose(0, 2, 1)


def _sample_attend_kernel(pt_ref, sel_ref, slopes_ref, q_ref, kn_ref, vn_ref, ck_hbm, cv_hbm, o_ref,
                          kbuf, vbuf, sem, *, layer, n_heads, n_pages, pages_per_block, scale):
    b = pl.program_id(0)
    page, dh = kbuf.shape[2], kbuf.shape[3]
    n_sub = MOBA_TOPK * pages_per_block
    past_len = n_pages * page
    hi = lax.Precision.HIGHEST

    def block_of(h, s):
        return sel_ref[(b * n_heads + h) * MOBA_TOPK + s // pages_per_block]

    def page_copies(h, s):
        pg = pt_ref[b * n_pages + block_of(h, s) * pages_per_block + s % pages_per_block]
        return (pltpu.make_async_copy(ck_hbm.at[layer, pg, :, h, :], kbuf.at[h, s], sem.at[0, h]),
                pltpu.make_async_copy(cv_hbm.at[layer, pg, :, h, :], vbuf.at[h, s], sem.at[1, h]))

    for h in range(n_heads):
        for s in range(n_sub):
            for cp in page_copies(h, s):
                cp.start()

    lane = lax.broadcasted_iota(jnp.int32, (1, n_sub * page), 1)
    for h in range(n_heads):
        for s in range(n_sub):
            for cp in page_copies(h, s):
                cp.wait()
        q = q_ref[h:h + 1, :]
        q8 = jnp.broadcast_to(q, (8, dh))
        kh = kbuf[h].reshape(n_sub * page, dh)
        vh = vbuf[h].reshape(n_sub * page, dh)
        sc = lax.dot_general(q8, kh, (((1,), (1,)), ((), ())), precision=hi,
                             preferred_element_type=F32)[0:1, :] * scale
        pos = jnp.zeros((1, n_sub * page), jnp.int32)
        for s in range(n_sub):
            first = block_of(h, s) * MOBA_BLOCK + (s % pages_per_block) * page
            pos = jnp.where((lane >= s * page) & (lane < (s + 1) * page), first + (lane - s * page), pos)
        sc = sc - slopes_ref[h] * (past_len - pos).astype(F32)
        s_new = jnp.sum(q * kn_ref[h:h + 1, :], axis=-1, keepdims=True) * scale
        m = jnp.maximum(jnp.max(sc, axis=-1, keepdims=True), s_new)
        p = jnp.exp(sc - m)
        p_new = jnp.exp(s_new - m)
        denom = jnp.sum(p, axis=-1, keepdims=True) + p_new
        pv = jnp.dot(jnp.broadcast_to(p, (8, n_sub * page)), vh, precision=hi, preferred_element_type=F32)
        o_ref[h:h + 1, :] = (pv[0:1, :] + p_new * vn_ref[h:h + 1, :]) / denom


def _sample_attend(q_s, k_new, v_new, cache_k, cache_v, layer, page_table, sel, slopes, n_heads):
    bs, hd = q_s.shape
    dh = hd // n_heads
    page = cache_k.shape[2]
    n_pages = page_table.shape[1]
    ppb = MOBA_BLOCK // page
    n_sub = MOBA_TOPK * ppb
    row_spec = pl.BlockSpec((None, n_heads, dh), lambda b, pt, sl: (b, 0, 0))
    grid_spec = pltpu.PrefetchScalarGridSpec(
        num_scalar_prefetch=2,
        grid=(bs,),
        in_specs=[pl.BlockSpec(memory_space=pltpu.SMEM),
                  row_spec, row_spec, row_spec,
                  pl.BlockSpec(memory_space=pl.ANY),
                  pl.BlockSpec(memory_space=pl.ANY)],
        out_specs=pl.BlockSpec((None, n_heads, dh), lambda b, pt, sl: (b, 0, 0)),
        scratch_shapes=[pltpu.VMEM((n_heads, n_sub, page, dh), F32),
                        pltpu.VMEM((n_heads, n_sub, page, dh), F32),
                        pltpu.SemaphoreType.DMA((2, n_heads))],
    )
    out = pl.pallas_call(
        functools.partial(_sample_attend_kernel, layer=layer, n_heads=n_heads, n_pages=n_pages,
                          pages_per_block=ppb, scale=dh ** -0.5),
        grid_spec=grid_spec,
        out_shape=jax.ShapeDtypeStruct((bs, n_heads, dh), F32),
        compiler_params=_params(("arbitrary",)),
        name="sample_attend",
    )(page_table.reshape(-1), sel.reshape(-1), slopes,
      q_s.reshape(bs, n_heads, dh), k_new.reshape(bs, n_heads, dh), v_new.reshape(bs, n_heads, dh),
      cache_k, cache_v)
    return out.reshape(bs, hd)


def _router_kernel(x_ref, w_ref, b_ref, slab_ref, cnt_ref, carry_ref, *, n_tokens):
    i = pl.program_id(0)
    tm = x_ref.shape[0]

    @pl.when(i == 0)
    def _():
        carry_ref[...] = jnp.zeros_like(carry_ref)

    logits = jnp.dot(x_ref[...], w_ref[...], precision=lax.Precision.HIGHEST,
                     preferred_element_type=F32) + b_ref[...]
    lane = lax.broadcasted_iota(jnp.int32, logits.shape, 1)
    lane_f = lane.astype(F32)
    valid = (lax.broadcasted_iota(jnp.int32, (tm, 1), 0) + i * tm) < n_tokens
    vals, onehots = [], []
    work = logits
    for _ in range(TOP_K):
        mx = jnp.max(work, axis=-1, keepdims=True)
        idx = jnp.min(jnp.where(work == mx, lane_f, float(LANES)), axis=-1, keepdims=True)
        oh = lane_f == idx
        vals.append(mx)
        onehots.append(jnp.where(oh & valid, 1.0, 0.0))
        work = jnp.where(oh, -3e38, work)
    exps = [jnp.exp(v - vals[0]) for v in vals]
    denom = exps[0]
    for e in exps[1:]:
        denom = denom + e
    oh_all = onehots[0]
    for oh in onehots[1:]:
        oh_all = oh_all + oh
    r = lax.broadcasted_iota(jnp.int32, (tm, tm), 0)
    c = lax.broadcasted_iota(jnp.int32, (tm, tm), 1)
    tri = jnp.where(c < r, 1.0, 0.0).astype(BF16)
    before = jnp.dot(tri, oh_all.astype(BF16), preferred_element_type=F32) + carry_ref[...]
    slab = jnp.zeros(logits.shape, F32)
    for kk in range(TOP_K):
        idx_k = jnp.sum(onehots[kk] * lane_f, axis=-1, keepdims=True)
        rank_k = jnp.sum(onehots[kk] * before, axis=-1, keepdims=True)
        gate_k = jnp.where(valid, exps[kk] / denom, 0.0)
        slab = jnp.where(lane == kk, idx_k, slab)
        slab = jnp.where(lane == TOP_K + kk, gate_k, slab)
        slab = jnp.where(lane == 2 * TOP_K + kk, rank_k, slab)
    slab_ref[...] = slab
    carry_ref[...] += jnp.sum(oh_all, axis=0, keepdims=True)
    cnt_ref[...] = jnp.broadcast_to(carry_ref[...], cnt_ref.shape)


def _router(x, w, b, n_tokens):
    t, d = x.shape
    n_exp = w.shape[1]
    assert TOP_K <= n_exp <= LANES and 3 * TOP_K <= LANES
    w_pad = jnp.zeros((d, LANES), F32).at[:, :n_exp].set(w)
    b_pad = jnp.full((1, LANES), NEG_BIG, F32).at[0, :n_exp].set(b)
    return pl.pallas_call(
        functools.partial(_router_kernel, n_tokens=n_tokens),
        grid=(t // ROW_TILE,),
        in_specs=[pl.BlockSpec((ROW_TILE, d), lambda i: (i, 0)),
                  pl.BlockSpec((d, LANES), lambda i: (0, 0)),
                  pl.BlockSpec((1, LANES), lambda i: (0, 0))],
        out_specs=[pl.BlockSpec((ROW_TILE, LANES), lambda i: (i, 0)),
                   pl.BlockSpec((8, LANES), lambda i: (0, 0))],
        out_shape=[jax.ShapeDtypeStruct((t, LANES), F32), jax.ShapeDtypeStruct((8, LANES), F32)],
        scratch_shapes=[pltpu.VMEM((1, LANES), F32)],
        compiler_params=_params(("arbitrary",)),
        name="router",
    )(x, w_pad, b_pad)


def _expert_kernel(we_ref, ws_ref, wn_ref, tok_ref, dst_ref, x_hbm, wg_ref, wu_ref, wd_ref, bg_ref, bu_ref, bd_ref,
                   y_hbm, xbuf, xb_ref, acc_ref, gsem, ssem, *, n_chunks):
    del we_ref
    w = pl.program_id(0)
    c = pl.program_id(1)
    n_items = pl.num_programs(0)
    per_step = SUPER_TILE // n_chunks
    n_rows = wn_ref[w]
    slot = w % 2
    nxt = jnp.minimum(w + 1, n_items - 1)
    n_next = jnp.where(w + 1 < n_items, wn_ref[nxt], 0)
    prv = jnp.maximum(w - 1, 0)
    n_prev = jnp.where(w >= 1, wn_ref[prv], 0)
    n_prev2 = jnp.where(w >= 2, wn_ref[jnp.maximum(w - 2, 0)], 0)

    def gather_copy(r, tok):
        return pltpu.make_async_copy(x_hbm.at[pl.ds(tok, 1), :], xbuf.at[pl.ds(r, 1), :], gsem)

    def scatter_copy(sl, r, dst):
        return pltpu.make_async_copy(acc_ref.at[sl, pl.ds(r, 1), :], y_hbm.at[pl.ds(dst, 1), :], ssem.at[sl])

    def for_rows(lo, hi, body):
        groups = (hi - lo) // 8

        def step8(i, carry):
            for u in range(8):
                body(lo + i * 8 + u)
            return carry

        def step1(r, carry):
            body(r)
            return carry

        lax.fori_loop(0, groups, step8, 0)
        lax.fori_loop(lo + groups * 8, hi, step1, 0)

    @pl.when(c == 0)
    def _():
        @pl.when(w == 0)
        def _():
            xbuf[...] = jnp.zeros_like(xbuf)
            for_rows(0, n_rows, lambda r: gather_copy(r, tok_ref[ws_ref[0] + r]).start())

        for_rows(0, n_rows, lambda r: gather_copy(r, 0).wait())
        for_rows(0, n_prev2, lambda r: scatter_copy(slot, r, 0).wait())
        xb_ref[...] = xbuf[...].astype(BF16)
        acc_ref[slot] = jnp.broadcast_to(bd_ref[...], acc_ref.shape[1:])

    n_sub = (n_rows + SUB_TILE - 1) // SUB_TILE
    half = acc_ref.shape[2] // 2
    for k in range(1, SUPER_TILE // SUB_TILE + 1):
        rows = slice(0, k * SUB_TILE)

        @pl.when(n_sub == k)
        def _():
            xb = xb_ref[rows, :]
            gate = jnp.dot(xb, wg_ref[...].astype(BF16), preferred_element_type=F32) + bg_ref[...]
            up = jnp.dot(xb, wu_ref[...].astype(BF16), preferred_element_type=F32) + bu_ref[...]
            gate = jnp.minimum(gate, SWIGLU_LIMIT)
            up = jnp.clip(up, -SWIGLU_LIMIT, SWIGLU_LIMIT)
            act = (gate * (1.0 / (1.0 + jnp.exp(-SWIGLU_ALPHA * gate))) * (up + 1.0)).astype(BF16)
            for cols in (slice(0, half), slice(half, 2 * half)):
                acc_ref[slot, rows, cols] += jnp.dot(act, wd_ref[:, cols].astype(BF16), preferred_element_type=F32)

    first = c * per_step
    base_next = ws_ref[nxt]
    base_prev = ws_ref[prv]
    for_rows(jnp.minimum(first, n_next), jnp.minimum(first + per_step, n_next),
             lambda r: gather_copy(r, tok_ref[base_next + r]).start())
    for_rows(jnp.minimum(first, n_prev), jnp.minimum(first + per_step, n_prev),
             lambda r: scatter_copy(1 - slot, r, dst_ref[base_prev + r]).start())

    @pl.when((w == n_items - 1) & (c == n_chunks - 1))
    def _():
        for_rows(0, n_prev, lambda r: scatter_copy(1 - slot, r, 0).wait())


def _experts(x, item_expert, item_start, item_rows, tok, dst, w_gate_up, b_gate_up, w_down, b_down, layer):
    t, d = x.shape
    n_exp, _, two_de = w_gate_up.shape[1:]
    de = two_de // 2
    tn = _pick_tile(de, 256, LANES)
    nc = de // tn
    n_items = item_expert.shape[0]
    assert SUPER_TILE % nc == 0 and n_items >= 2
    bgu = b_gate_up.reshape(b_gate_up.shape[0], n_exp, 1, two_de)
    bdn = b_down.reshape(b_down.shape[0], n_exp, 1, d)

    def chunk(i, c, wn):
        return jnp.where(wn[i] > 0, c, nc - 1)

    def wspec(shape, index):
        return pl.BlockSpec(shape, lambda i, c, we, ws, wn, tk, ds: index(we[i], chunk(i, c, wn)))

    grid_spec = pltpu.PrefetchScalarGridSpec(
        num_scalar_prefetch=5,
        grid=(n_items, nc),
        in_specs=[pl.BlockSpec(memory_space=pl.ANY),
                  wspec((None, None, d, tn), lambda e, c: (layer, e, 0, c)),
                  wspec((None, None, d, tn), lambda e, c: (layer, e, 0, nc + c)),
                  wspec((None, None, tn, d), lambda e, c: (layer, e, c, 0)),
                  wspec((None, None, 1, tn), lambda e, c: (layer, e, 0, c)),
                  wspec((None, None, 1, tn), lambda e, c: (layer, e, 0, nc + c)),
                  wspec((None, None, 1, d), lambda e, c: (layer, e, 0, 0))],
        out_specs=pl.BlockSpec(memory_space=pl.ANY),
        scratch_shapes=[pltpu.VMEM((SUPER_TILE, d), F32),
                        pltpu.VMEM((SUPER_TILE, d), BF16),
                        pltpu.VMEM((2, SUPER_TILE, d), F32),
                        pltpu.SemaphoreType.DMA(()),
                        pltpu.SemaphoreType.DMA((2,))],
    )
    return pl.pallas_call(
        functools.partial(_expert_kernel, n_chunks=nc),
        grid_spec=grid_spec,
        out_shape=jax.ShapeDtypeStruct((TOP_K * t, d), F32),
        compiler_params=_params(("arbitrary", "arbitrary")),
        name="experts",
    )(item_expert, item_start, item_rows, tok, dst, x, w_gate_up, w_gate_up, w_down, bgu, bgu, bdn)


def _sample_expert_kernel(el_ref, nu_ref, x_ref, comb_ref, wg_ref, wu_ref, wd_ref, bg_ref, bu_ref, bd_ref,
                          g_ref, b_ref, o_ref, acc_ref, *, alpha):
    del el_ref
    j = pl.program_id(0)
    c = pl.program_id(1)

    @pl.when((j == 0) & (c == 0))
    def _():
        acc_ref[...] = jnp.zeros_like(acc_ref)

    @pl.when(j < nu_ref[0])
    def _():
        x = x_ref[...]
        gate = jnp.minimum(_dot3(x, wg_ref[...]) + bg_ref[...], SWIGLU_LIMIT)
        up = jnp.clip(_dot3(x, wu_ref[...]) + bu_ref[...], -SWIGLU_LIMIT, SWIGLU_LIMIT)
        act = gate * (1.0 / (1.0 + jnp.exp(-SWIGLU_ALPHA * gate))) * (up + 1.0)
        part = _dot3(act, wd_ref[...])
        part = part + jnp.where(c == 0, 1.0, 0.0) * bd_ref[...]
        acc_ref[...] += comb_ref[:, 0:1] * part

    @pl.when((j == pl.num_programs(0) - 1) & (c == pl.num_programs(1) - 1))
    def _():
        o_ref[...] = _layer_norm(alpha * x_ref[...] + acc_ref[...], g_ref[...], b_ref[...])


def _sample_moe_layer(x, layer, router_w, router_b, w_gate_up, b_gate_up, w_down, b_down, ln_g, ln_b, alpha):
    m, d = x.shape
    n_exp = router_w.shape[1]
    two_de = w_gate_up.shape[3]
    de = two_de // 2
    x_pad = jnp.zeros((ROW_TILE, d), F32).at[:m].set(x)
    slab, _ = _router(x_pad, router_w, router_b, m)
    idx = slab[:m, :TOP_K].astype(jnp.int32)
    gates = slab[:m, TOP_K:2 * TOP_K]
    comb = jnp.einsum("tk,tke->te", gates, jax.nn.one_hot(idx, n_exp, dtype=F32))
    used = jnp.any(comb > 0.0, axis=0)
    n_used = jnp.sum(used).astype(jnp.int32)
    order = jnp.argsort(jnp.logical_not(used), stable=True).astype(jnp.int32)
    n_slots = min(n_exp, m * TOP_K)
    slots = jnp.minimum(jnp.arange(n_slots, dtype=jnp.int32), n_used - 1)
    e_list = order[slots]
    comb_b = jnp.broadcast_to(comb.T[:, :, None], (n_exp, m, LANES))
    tn = _pick_tile(de, 256, LANES)
    nc = de // tn
    bgu = b_gate_up.reshape(b_gate_up.shape[0], n_exp, 1, two_de)
    bdn = b_down.reshape(b_down.shape[0], n_exp, 1, d)

    def chunk(j, c, nu):
        return jnp.where(j < nu[0], c, nc - 1)

    row = pl.BlockSpec((1, d), lambda j, c, el, nu: (0, 0))
    grid_spec = pltpu.PrefetchScalarGridSpec(
        num_scalar_prefetch=2,
        grid=(n_slots, nc),
        in_specs=[pl.BlockSpec((m, d), lambda j, c, el, nu: (0, 0)),
                  pl.BlockSpec((None, m, LANES), lambda j, c, el, nu: (el[j], 0, 0)),
                  pl.BlockSpec((None, None, d, tn), lambda j, c, el, nu: (layer, el[j], 0, chunk(j, c, nu))),
                  pl.BlockSpec((None, None, d, tn), lambda j, c, el, nu: (layer, el[j], 0, nc + chunk(j, c, nu))),
                  pl.BlockSpec((None, None, tn, d), lambda j, c, el, nu: (layer, el[j], chunk(j, c, nu), 0)),
                  pl.BlockSpec((None, None, 1, tn), lambda j, c, el, nu: (layer, el[j], 0, chunk(j, c, nu))),
                  pl.BlockSpec((None, None, 1, tn), lambda j, c, el, nu: (layer, el[j], 0, nc + chunk(j, c, nu))),
                  pl.BlockSpec((None, None, 1, d), lambda j, c, el, nu: (layer, el[j], 0, 0)),
                  row, row],
        out_specs=pl.BlockSpec((m, d), lambda j, c, el, nu: (0, 0)),
        scratch_shapes=[pltpu.VMEM((m, d), F32)],
    )
    return pl.pallas_call(
        functools.partial(_sample_expert_kernel, alpha=alpha),
        grid_spec=grid_spec,
        out_shape=jax.ShapeDtypeStruct((m, d), F32),
        compiler_params=_params(("arbitrary", "arbitrary")),
        name="sample_experts",
    )(e_list, n_used.reshape(1), x, comb_b, w_gate_up, w_gate_up, w_down, bgu, bgu, bdn,
      ln_g.reshape(1, d), ln_b.reshape(1, d))


def _combine_ln_kernel(*refs, alpha):
    y_refs = refs[:TOP_K]
    slab_ref, x_ref, g_ref, b_ref, o_ref = refs[TOP_K:]
    f = alpha * x_ref[...]
    for kk in range(TOP_K):
        f = f + slab_ref[:, TOP_K + kk:TOP_K + kk + 1] * y_refs[kk][...]
    o_ref[...] = _layer_norm(f, g_ref[...], b_ref[...])


def _combine_ln(y4, slab, x, g, b, alpha):
    t, d = x.shape
    nt = t // ROW_TILE

    def y_spec(kk):
        return pl.BlockSpec((ROW_TILE, d), lambda i: (kk * nt + i, 0))

    return pl.pallas_call(
        functools.partial(_combine_ln_kernel, alpha=alpha),
        grid=(nt,),
        in_specs=[y_spec(kk) for kk in range(TOP_K)] + [
            pl.BlockSpec((ROW_TILE, LANES), lambda i: (i, 0)),
            pl.BlockSpec((ROW_TILE, d), lambda i: (i, 0)),
            pl.BlockSpec((1, d), lambda i: (0, 0)),
            pl.BlockSpec((1, d), lambda i: (0, 0))],
        out_specs=pl.BlockSpec((ROW_TILE, d), lambda i: (i, 0)),
        out_shape=jax.ShapeDtypeStruct((t, d), F32),
        compiler_params=_params(("arbitrary",)),
        name="combine_ln",
    )(*([y4] * TOP_K), slab, x, g.reshape(1, d), b.reshape(1, d))


def _moe_layer(x, layer, router_w, router_b, w_gate_up, b_gate_up, w_down, b_down, ln_g, ln_b, alpha):
    t, d = x.shape
    n_exp = router_w.shape[1]
    slab, cnt = _router(x, router_w, router_b, t)
    idx = slab[:, :TOP_K].astype(jnp.int32)
    rank = slab[:, 2 * TOP_K:3 * TOP_K].astype(jnp.int32)
    counts = cnt[0, :n_exp].astype(jnp.int32)
    first_row = jnp.cumsum(counts) - counts
    pos = first_row[idx] + rank
    pair_id = jnp.arange(t * TOP_K, dtype=jnp.int32).reshape(t, TOP_K)
    pairs = jnp.zeros((t * TOP_K,), jnp.int32).at[pos.reshape(-1)].set(pair_id.reshape(-1))
    tok = pairs // TOP_K
    dst = (pairs % TOP_K) * t + tok
    n_items = (t * TOP_K) // SUPER_TILE + n_exp
    items_per_exp = (counts + SUPER_TILE - 1) // SUPER_TILE
    item_end = jnp.cumsum(items_per_exp)
    ids = jnp.arange(n_items, dtype=jnp.int32)
    live = ids < item_end[-1]
    item_expert = jnp.minimum(jnp.searchsorted(item_end, jnp.minimum(ids, item_end[-1] - 1), side="right"),
                              n_exp - 1).astype(jnp.int32)
    local = ids - (item_end - items_per_exp)[item_expert]
    item_start = jnp.where(live, first_row[item_expert] + local * SUPER_TILE, 0).astype(jnp.int32)
    item_rows = jnp.where(live, jnp.clip(counts[item_expert] - local * SUPER_TILE, 0, SUPER_TILE), 0).astype(jnp.int32)
    y4 = _experts(x, item_expert, item_start, item_rows, tok, dst, w_gate_up, b_gate_up, w_down, b_down, layer)
    return _combine_ln(y4, slab, x, ln_g, ln_b, alpha)


def _gmlp_prompt(x, w_in, b_in, lnv_g, lnv_b, w_s, b_s, w_out, ln_g, ln_b, alpha):
    z = _mm_bias(x, w_in, b_in, gelu=True)
    causal = jnp.tril(jnp.ones((CHUNK, CHUNK), dtype=bool))
    vn, gated = _spatial_mix(z, jnp.where(causal[None], w_s, 0.0), b_s.T, lnv_g, lnv_b)
    return _mm_resid_ln(gated, w_out, x, ln_g, ln_b, alpha), vn


def _gmlp_sample(x, w_in, b_in, lnv_g, lnv_b, w_s, b_s, w_out, ln_g, ln_b, alpha):
    z = _mm_bias(x, w_in, b_in, gelu=True, hi=True)
    group_dim = lnv_g.shape[0] // w_s.shape[0]
    vn, gated = _sample_gate(z, jnp.repeat(w_s[:, 0, 0], group_dim), jnp.repeat(b_s[:, 0], group_dim), lnv_g, lnv_b)
    return _mm_resid_ln(gated, w_out, x, ln_g, ln_b, alpha, hi=True), vn


def kernel(x_prompt, x_sample, cache_k, cache_v, page_table, w_in_a, b_in_a, ln_v_g, ln_v_b, w_s, b_s, w_out_a, w_qkv, w_o, ln_mix_g, ln_mix_b, ln_ffn_g, ln_ffn_b, router_w, router_b, w_gate_up, b_gate_up, w_down, b_down):
    bp, sp, d = x_prompt.shape
    bs, ss, _ = x_sample.shape
    depth = ln_mix_g.shape[0]
    n_heads = cache_k.shape[3]
    dh = cache_k.shape[4]
    page = cache_k.shape[2]
    n_pages = page_table.shape[1]
    assert ss == 1 and sp % CHUNK == 0 and sp % MOBA_BLOCK == 0
    assert MOBA_BLOCK % page == 0 and (n_pages * page) % MOBA_BLOCK == 0
    alpha = (2 * depth) ** 0.25
    n_prompt = bp * sp
    n_tokens = n_prompt + bs
    slopes = 2.0 ** (-8.0 * jnp.arange(1, n_heads + 1, dtype=F32) / n_heads)
    no_bias = jnp.zeros((3 * d,), F32)

    xp = x_prompt.reshape(n_prompt, d)
    xs = x_sample.reshape(bs, d)
    v_rows_p, v_rows_s, k_p, v_p, k_s, v_s = [], [], [], [], [], []
    for i in range(depth):
        li = i // 2
        g_mix, b_mix = ln_mix_g[i], ln_mix_b[i]
        if i % 2 == 0:
            mixer = (w_in_a[li], b_in_a[li], ln_v_g[li], ln_v_b[li], w_s[li], b_s[li], w_out_a[li], g_mix, b_mix, alpha)
            xp, vn_p = _gmlp_prompt(xp, *mixer)
            xs, vn_s = _gmlp_sample(xs, *mixer)
            last0 = ((sp - 1) // CHUNK) * CHUNK
            v_rows_p.append(vn_p.reshape(bp, sp, -1)[:, last0:])
            v_rows_s.append(vn_s.reshape(bs, ss, -1))
        else:
            qkv = _mm_bias(xp, w_qkv[li], no_bias, gelu=False)
            attn_p = _moba_prompt(qkv, slopes, bp, sp, n_heads)
            xp = _mm_resid_ln(attn_p, w_o[li], xp, g_mix, b_mix, alpha)
            k_p.append(qkv[:, d:2 * d].reshape(bp, sp, n_heads, dh))
            v_p.append(qkv[:, 2 * d:].reshape(bp, sp, n_heads, dh))

            qkv_s = _mm_bias(xs, w_qkv[li], no_bias, gelu=False, hi=True)
            q_s, k_new, v_new = qkv_s[:, :d], qkv_s[:, d:2 * d], qkv_s[:, 2 * d:]
            ksum = _past_block_sums(cache_k, li, page_table)
            sel = _sample_select(q_s, ksum, n_heads)
            attn_s = _sample_attend(q_s, k_new, v_new, cache_k, cache_v, li, page_table, sel, slopes, n_heads)
            xs = _mm_resid_ln(attn_s, w_o[li], xs, g_mix, b_mix, alpha, hi=True)
            k_s.append(k_new.reshape(bs, ss, n_heads, dh))
            v_s.append(v_new.reshape(bs, ss, n_heads, dh))
        moe = (router_w[i], router_b[i], w_gate_up, b_gate_up, w_down, b_down, ln_ffn_g[i], ln_ffn_b[i], alpha)
        if i < depth - 1:
            xp = _moe_layer(xp, i, *moe)
            xs = _sample_moe_layer(xs, i, *moe)
        else:
            t = _round_up(n_tokens, ROW_TILE)
            x = _moe_layer(jnp.concatenate([xp, xs, jnp.zeros((t - n_tokens, d), F32)], axis=0), i, *moe)
            xp, xs = x[:n_prompt], x[n_prompt:n_tokens]
    return (xp.reshape(bp, sp, d), xs.reshape(bs, ss, d), jnp.stack(v_rows_p), jnp.stack(v_rows_s),
            jnp.stack(k_p), jnp.stack(v_p), jnp.stack(k_s), jnp.stack(v_s))
```

```python
import functools

import jax
import jax.numpy as jnp
from jax import lax
from jax.experimental import pallas as pl
from jax.experimental.pallas import tpu as pltpu

CHUNK = 128
MOBA_BLOCK = 256
MOBA_TOPK = 3
TOP_K = 4
SWIGLU_LIMIT = 7.0
SWIGLU_ALPHA = 1.702
LN_EPS = 1e-5
NEG_BIG = -1e30
LANES = 128
ROW_TILE = 256
SUPER_TILE = 1152
SUB_TILE = 384
VMEM_LIMIT = 56 * 1024 * 1024

F32 = jnp.float32
BF16 = jnp.bfloat16


def _round_up(n, m):
    return -(-n // m) * m


def _pick_tile(n, limit, mult):
    best = None
    for t in range(mult, min(n, limit) + 1, mult):
        if n % t == 0:
            best = t
    assert best is not None, (n, limit, mult)
    return best


def _params(sem):
    return pltpu.CompilerParams(dimension_semantics=sem, vmem_limit_bytes=VMEM_LIMIT)


def _layer_norm(y, g, b):
    mu = jnp.mean(y, axis=-1, keepdims=True)
    d = y - mu
    var = jnp.mean(d * d, axis=-1, keepdims=True)
    return d * lax.rsqrt(var + LN_EPS) * g + b


def _split_bf16(x):
    hi = x.astype(BF16)
    return hi, (x - hi.astype(F32)).astype(BF16)


def _dot3(x, w):
    xh, xl = _split_bf16(x)
    wh, wl = _split_bf16(w)
    m = x.shape[0]
    r = jnp.dot(jnp.concatenate([xh, xl], axis=0), wh, preferred_element_type=F32)
    return r[:m] + r[m:] + jnp.dot(xh, wl, preferred_element_type=F32)


def _mm_gelu_kernel(x_ref, w_ref, b_ref, o_ref, *scratch, hi, gelu):
    if hi:
        h = _dot3(x_ref[...], w_ref[...])
    else:
        xb_ref, = scratch

        @pl.when(pl.program_id(1) == 0)
        def _():
            xb_ref[...] = x_ref[...].astype(BF16)

        h = jnp.dot(xb_ref[...], w_ref[...].astype(BF16), preferred_element_type=F32)
    h = h + b_ref[...]
    o_ref[...] = 0.5 * h * (1.0 + lax.erf(h * (2.0 ** -0.5))) if gelu else h


def _mm_bias(x, w, b, *, gelu, hi=False):
    m, k = x.shape
    n = w.shape[1]
    tm = _pick_tile(m, 1024, 8)
    tn = _pick_tile(n, 512, LANES)
    return pl.pallas_call(
        functools.partial(_mm_gelu_kernel, hi=hi, gelu=gelu),
        grid=(m // tm, n // tn),
        in_specs=[pl.BlockSpec((tm, k), lambda i, j: (i, 0)),
                  pl.BlockSpec((k, tn), lambda i, j: (0, j)),
                  pl.BlockSpec((1, tn), lambda i, j: (0, j))],
        out_specs=pl.BlockSpec((tm, tn), lambda i, j: (i, j)),
        out_shape=jax.ShapeDtypeStruct((m, n), F32),
        scratch_shapes=[] if hi else [pltpu.VMEM((tm, k), BF16)],
        compiler_params=_params(("arbitrary", "arbitrary")),
        name="mm_bias_hi" if hi else "mm_bias",
    )(x, w, b.reshape(1, n))


def _mm_resid_ln_kernel(a_ref, w_ref, x_ref, g_ref, b_ref, o_ref, acc_ref, *, alpha, hi):
    kk = pl.program_id(1)
    if hi:
        part = _dot3(a_ref[...], w_ref[...])
    else:
        part = jnp.dot(a_ref[...].astype(BF16), w_ref[...].astype(BF16), preferred_element_type=F32)

    @pl.when(kk == 0)
    def _():
        acc_ref[...] = part

    @pl.when(kk > 0)
    def _():
        acc_ref[...] += part

    @pl.when(kk == pl.num_programs(1) - 1)
    def _():
        o_ref[...] = _layer_norm(alpha * x_ref[...] + acc_ref[...], g_ref[...], b_ref[...])


def _mm_resid_ln(a, w, x, g, b, alpha, hi=False):
    m, k = a.shape
    n = w.shape[1]
    tm = _pick_tile(m, 512, 8)
    tk = _pick_tile(k, 512, LANES)
    return pl.pallas_call(
        functools.partial(_mm_resid_ln_kernel, alpha=alpha, hi=hi),
        grid=(m // tm, k // tk),
        in_specs=[pl.BlockSpec((tm, tk), lambda i, j: (i, j)),
                  pl.BlockSpec((tk, n), lambda i, j: (j, 0)),
                  pl.BlockSpec((tm, n), lambda i, j: (i, 0)),
                  pl.BlockSpec((1, n), lambda i, j: (0, 0)),
                  pl.BlockSpec((1, n), lambda i, j: (0, 0))],
        out_specs=pl.BlockSpec((tm, n), lambda i, j: (i, 0)),
        out_shape=jax.ShapeDtypeStruct((m, n), F32),
        scratch_shapes=[pltpu.VMEM((tm, n), F32)],
        compiler_params=_params(("arbitrary", "arbitrary")),
        name="mm_resid_ln_hi" if hi else "mm_resid_ln",
    )(a, w, x, g.reshape(1, n), b.reshape(1, n))


def _spatial_mix_kernel(u_ref, v_ref, ws_ref, bs_ref, g_ref, b_ref, vn_ref, o_ref, *, n_groups):
    vn = _layer_norm(v_ref[...], g_ref[...], b_ref[...])
    vn_ref[...] = vn
    vb = vn.astype(BF16)
    gd = vn.shape[1] // n_groups
    for gi in range(n_groups):
        sl = slice(gi * gd, (gi + 1) * gd)
        mixed = jnp.dot(ws_ref[gi].astype(BF16), vb[:, sl], preferred_element_type=F32)
        mixed = mixed + bs_ref[:, gi:gi + 1]
        o_ref[:, sl] = (u_ref[:, sl] * mixed).astype(BF16)


def _spatial_mix(z, ws_causal, bs_t, ln_g, ln_b):
    t, two_da = z.shape
    da = two_da // 2
    n_groups = ws_causal.shape[0]
    return pl.pallas_call(
        functools.partial(_spatial_mix_kernel, n_groups=n_groups),
        grid=(t // CHUNK,),
        in_specs=[pl.BlockSpec((CHUNK, da), lambda i: (i, 0)),
                  pl.BlockSpec((CHUNK, da), lambda i: (i, 1)),
                  pl.BlockSpec((n_groups, CHUNK, CHUNK), lambda i: (0, 0, 0)),
                  pl.BlockSpec((CHUNK, n_groups), lambda i: (0, 0)),
                  pl.BlockSpec((1, da), lambda i: (0, 0)),
                  pl.BlockSpec((1, da), lambda i: (0, 0))],
        out_specs=[pl.BlockSpec((CHUNK, da), lambda i: (i, 0)),
                   pl.BlockSpec((CHUNK, da), lambda i: (i, 0))],
        out_shape=[jax.ShapeDtypeStruct((t, da), F32), jax.ShapeDtypeStruct((t, da), BF16)],
        compiler_params=_params(("arbitrary",)),
        name="spatial_mix",
    )(z, z, ws_causal, bs_t, ln_g.reshape(1, da), ln_b.reshape(1, da))


def _sample_gate_kernel(u_ref, v_ref, w_ref, c_ref, g_ref, b_ref, vn_ref, o_ref):
    vn = _layer_norm(v_ref[...], g_ref[...], b_ref[...])
    vn_ref[...] = vn
    o_ref[...] = u_ref[...] * (vn * w_ref[...] + c_ref[...])


def _sample_gate(z, w00, b0, ln_g, ln_b):
    m, two_da = z.shape
    da = two_da // 2
    row = pl.BlockSpec((1, da), lambda i: (0, 0))
    return pl.pallas_call(
        _sample_gate_kernel,
        grid=(1,),
        in_specs=[pl.BlockSpec((m, da), lambda i: (0, 0)), pl.BlockSpec((m, da), lambda i: (0, 1)),
                  row, row, row, row],
        out_specs=[pl.BlockSpec((m, da), lambda i: (0, 0)), pl.BlockSpec((m, da), lambda i: (0, 0))],
        out_shape=[jax.ShapeDtypeStruct((m, da), F32), jax.ShapeDtypeStruct((m, da), F32)],
        compiler_params=_params(("arbitrary",)),
        name="sample_gate",
    )(z, z, w00.reshape(1, da), b0.reshape(1, da), ln_g.reshape(1, da), ln_b.reshape(1, da))


def _block_ranks(g, n_valid, n_blocks):
    blk = lax.broadcasted_iota(jnp.int32, g.shape, 0)
    past = blk < n_valid
    rows = []
    for n in range(n_blocks):
        gn = g[n:n + 1, :]
        beats = jnp.where(past & ((g > gn) | ((g == gn) & (blk < n))), 1.0, 0.0)
        rows.append(jnp.sum(beats, axis=0, keepdims=True))
    return jnp.concatenate(rows, axis=0)


def _moba_prompt_kernel(slopes_ref, q_ref, k_ref, v_ref, o_ref,
                        kb_ref, vt_ref, kmean_ref, bias_ref, sel_ref, acc_ref, *, n_blocks, scale):
    h = pl.program_id(1)
    qi = pl.program_id(2)
    slope = slopes_ref[h]
    blk = MOBA_BLOCK
    dh = q_ref.shape[1]
    krow = lax.broadcasted_iota(jnp.int32, (blk, blk), 0)
    qcol = lax.broadcasted_iota(jnp.int32, (blk, blk), 1)

    @pl.when(qi == 0)
    def _():
        kmean_ref[...] = jnp.zeros_like(kmean_ref)
        for n in range(n_blocks):
            kn = k_ref[n * blk:(n + 1) * blk, :]
            kb_ref[n] = kn.astype(BF16)
            vt_ref[n] = v_ref[n * blk:(n + 1) * blk, :].T.astype(BF16)
            kmean_ref[n:n + 1, :] = jnp.mean(kn, axis=0, keepdims=True)
        bias_ref[...] = slope * (qcol - krow).astype(F32)

    q_t = q_ref[...].T
    gate = jnp.dot(kmean_ref[...], q_t, precision=lax.Precision.HIGHEST, preferred_element_type=F32)
    rank = _block_ranks(gate, qi, n_blocks)
    bidx = lax.broadcasted_iota(jnp.int32, gate.shape, 0)
    chosen = (rank[0:n_blocks] < MOBA_TOPK) & (bidx[0:n_blocks] < qi)
    sel_ref[0:n_blocks, :] = (jnp.where(chosen, 0.0, NEG_BIG)
                              - slope * ((qi - bidx[0:n_blocks]) * blk).astype(F32))
    qb = (q_t * scale).astype(BF16)

    s = jnp.dot(kb_ref[qi], qb, preferred_element_type=F32) - bias_ref[...]
    s = jnp.where(krow <= qcol, s, NEG_BIG)
    m0 = jnp.max(s, axis=0, keepdims=True)
    p = jnp.exp(s - m0)
    l0 = jnp.sum(p, axis=0, keepdims=True)
    acc_ref[...] = jnp.dot(vt_ref[qi], p.astype(BF16), preferred_element_type=F32)

    def masked_scores(j):
        return jnp.dot(kb_ref[j], qb, preferred_element_type=F32) - bias_ref[...] + sel_ref[pl.ds(j, 1), :]

    def body(i, carry):
        m_old, l_old = carry
        ja, jb = 2 * i, 2 * i + 1
        sa, sb = masked_scores(ja), masked_scores(jb)
        m_new = jnp.maximum(m_old, jnp.maximum(jnp.max(sa, axis=0, keepdims=True),
                                               jnp.max(sb, axis=0, keepdims=True)))
        a = jnp.exp(m_old - m_new)
        pa = jnp.exp(sa - m_new)
        pb = jnp.exp(sb - m_new)
        acc_ref[...] = (a * acc_ref[...] + jnp.dot(vt_ref[ja], pa.astype(BF16), preferred_element_type=F32)
                        + jnp.dot(vt_ref[jb], pb.astype(BF16), preferred_element_type=F32))
        return m_new, a * l_old + jnp.sum(pa, axis=0, keepdims=True) + jnp.sum(pb, axis=0, keepdims=True)

    _, l_fin = lax.fori_loop(0, (qi + 1) // 2, body, (m0, l0))
    o_ref[...] = (acc_ref[...] / l_fin).T


def _moba_prompt(qkv, slopes, bp, sp, n_heads):
    d = qkv.shape[1] // 3
    dh = d // n_heads
    assert sp % MOBA_BLOCK == 0 and dh % LANES == 0
    nq = sp // MOBA_BLOCK
    nq_pad = _round_up(nq, 8)
    return pl.pallas_call(
        functools.partial(_moba_prompt_kernel, n_blocks=nq, scale=dh ** -0.5),
        grid=(bp, n_heads, nq),
        in_specs=[pl.BlockSpec(memory_space=pltpu.SMEM),
                  pl.BlockSpec((MOBA_BLOCK, dh), lambda b, h, i: (b * nq + i, h)),
                  pl.BlockSpec((sp, dh), lambda b, h, i: (b, n_heads + h)),
                  pl.BlockSpec((sp, dh), lambda b, h, i: (b, 2 * n_heads + h))],
        out_specs=pl.BlockSpec((MOBA_BLOCK, dh), lambda b, h, i: (b * nq + i, h)),
        out_shape=jax.ShapeDtypeStruct((bp * sp, d), F32),
        scratch_shapes=[pltpu.VMEM((nq, MOBA_BLOCK, dh), BF16),
                        pltpu.VMEM((nq, dh, MOBA_BLOCK), BF16),
                        pltpu.VMEM((nq_pad, dh), F32),
                        pltpu.VMEM((MOBA_BLOCK, MOBA_BLOCK), F32),
                        pltpu.VMEM((nq_pad, MOBA_BLOCK), F32),
                        pltpu.VMEM((dh, MOBA_BLOCK), F32)],
        compiler_params=_params(("arbitrary", "arbitrary", "arbitrary")),
        name="moba_prompt",
    )(slopes, qkv, qkv, qkv)


def _page_sum_kernel(pt_ref, *refs):
    del pt_ref
    *page_refs, o_ref = refs
    s = jnp.sum(page_refs[0][...], axis=0)
    for page_ref in page_refs[1:]:
        s = s + jnp.sum(page_ref[...], axis=0)
    o_ref[...] = s


def _past_block_sums(cache, layer, page_table):
    _, _, page, n_heads, dh = cache.shape
    bs, n_pages = page_table.shape
    ppb = MOBA_BLOCK // page
    nblk = n_pages // ppb

    def page_spec(u):
        return pl.BlockSpec((None, None, page, n_heads, dh),
                            lambda b, r, pt: (layer, pt[b * n_pages + r * ppb + u], 0, 0, 0))

    grid_spec = pltpu.PrefetchScalarGridSpec(
        num_scalar_prefetch=1,
        grid=(bs, nblk),
        in_specs=[page_spec(u) for u in range(ppb)],
        out_specs=pl.BlockSpec((None, None, n_heads, dh), lambda b, r, pt: (b, r, 0, 0)),
    )
    return pl.pallas_call(
        _page_sum_kernel,
        grid_spec=grid_spec,
        out_shape=jax.ShapeDtypeStruct((bs, nblk, n_heads, dh), F32),
        compiler_params=_params(("arbitrary", "arbitrary")),
        name="page_sum",
    )(page_table.reshape(-1), *([cache] * ppb))


def _sample_select_kernel(q_ref, ksum_ref, seg_ref, o_ref, *, n_blocks):
    prod = ksum_ref[...] * (1.0 / MOBA_BLOCK) * q_ref[...]
    gate = jnp.dot(prod, seg_ref[...], precision=lax.Precision.HIGHEST, preferred_element_type=F32)
    rank = _block_ranks(gate, n_blocks, n_blocks)
    bidx = lax.broadcasted_iota(jnp.int32, gate.shape, 0).astype(F32)
    rows = []
    for r in range(MOBA_TOPK):
        rows.append(jnp.sum(jnp.where(rank == float(r), bidx, 0.0), axis=0, keepdims=True))
    rows.append(jnp.zeros((8 - MOBA_TOPK, gate.shape[1]), F32))
    o_ref[...] = jnp.concatenate(rows, axis=0).astype(jnp.int32)


def _sample_select(q_s, ksum, n_heads):
    bs, nblk, _, dh = ksum.shape
    hd = n_heads * dh
    ksum = ksum.reshape(bs, nblk, hd)
    assert nblk >= MOBA_TOPK and nblk % 8 == 0 and n_heads <= LANES
    seg = (jnp.arange(hd)[:, None] // dh == jnp.arange(LANES)[None, :]).astype(F32)
    out = pl.pallas_call(
        functools.partial(_sample_select_kernel, n_blocks=nblk),
        grid=(bs,),
        in_specs=[pl.BlockSpec((None, 1, hd), lambda b: (b, 0, 0)),
                  pl.BlockSpec((None, nblk, hd), lambda b: (b, 0, 0)),
                  pl.BlockSpec((hd, LANES), lambda b: (0, 0))],
        out_specs=pl.BlockSpec((None, 8, LANES), lambda b: (b, 0, 0)),
        out_shape=jax.ShapeDtypeStruct((bs, 8, LANES), jnp.int32),
        compiler_params=_params(("arbitrary",)),
        name="sample_select",
    )(q_s.reshape(bs, 1, hd), ksum, seg)
    return out[:, :MOBA_TOPK, :n_heads].transpose(0, 2, 1)


def _sample_attend_kernel(pt_ref, sel_ref, slopes_ref, q_ref, kn_ref, vn_ref, ck_hbm, cv_hbm, o_ref,
                          kbuf, vbuf, sem, *, layer, n_heads, n_pages, pages_per_block, scale):
    b = pl.program_id(0)
    page, dh = kbuf.shape[2], kbuf.shape[3]
    n_sub = MOBA_TOPK * pages_per_block
    past_len = n_pages * page
    hi = lax.Precision.HIGHEST

    def block_of(h, s):
        return sel_ref[(b * n_heads + h) * MOBA_TOPK + s // pages_per_block]

    def page_copies(h, s):
        pg = pt_ref[b * n_pages + block_of(h, s) * pages_per_block + s % pages_per_block]
        return (pltpu.make_async_copy(ck_hbm.at[layer, pg, :, h, :], kbuf.at[h, s], sem.at[0, h]),
                pltpu.make_async_copy(cv_hbm.at[layer, pg, :, h, :], vbuf.at[h, s], sem.at[1, h]))

    for h in range(n_heads):
        for s in range(n_sub):
            for cp in page_copies(h, s):
                cp.start()

    lane = lax.broadcasted_iota(jnp.int32, (1, n_sub * page), 1)
    for h in range(n_heads):
        for s in range(n_sub):
            for cp in page_copies(h, s):
                cp.wait()
        q = q_ref[h:h + 1, :]
        q8 = jnp.broadcast_to(q, (8, dh))
        kh = kbuf[h].reshape(n_sub * page, dh)
        vh = vbuf[h].reshape(n_sub * page, dh)
        sc = lax.dot_general(q8, kh, (((1,), (1,)), ((), ())), precision=hi,
                             preferred_element_type=F32)[0:1, :] * scale
        pos = jnp.zeros((1, n_sub * page), jnp.int32)
        for s in range(n_sub):
            first = block_of(h, s) * MOBA_BLOCK + (s % pages_per_block) * page
            pos = jnp.where((lane >= s * page) & (lane < (s + 1) * page), first + (lane - s * page), pos)
        sc = sc - slopes_ref[h] * (past_len - pos).astype(F32)
        s_new = jnp.sum(q * kn_ref[h:h + 1, :], axis=-1, keepdims=True) * scale
        m = jnp.maximum(jnp.max(sc, axis=-1, keepdims=True), s_new)
        p = jnp.exp(sc - m)
        p_new = jnp.exp(s_new - m)
        denom = jnp.sum(p, axis=-1, keepdims=True) + p_new
        pv = jnp.dot(jnp.broadcast_to(p, (8, n_sub * page)), vh, precision=hi, preferred_element_type=F32)
        o_ref[h:h + 1, :] = (pv[0:1, :] + p_new * vn_ref[h:h + 1, :]) / denom


def _sample_attend(q_s, k_new, v_new, cache_k, cache_v, layer, page_table, sel, slopes, n_heads):
    bs, hd = q_s.shape
    dh = hd // n_heads
    page = cache_k.shape[2]
    n_pages = page_table.shape[1]
    ppb = MOBA_BLOCK // page
    n_sub = MOBA_TOPK * ppb
    row_spec = pl.BlockSpec((None, n_heads, dh), lambda b, pt, sl: (b, 0, 0))
    grid_spec = pltpu.PrefetchScalarGridSpec(
        num_scalar_prefetch=2,
        grid=(bs,),
        in_specs=[pl.BlockSpec(memory_space=pltpu.SMEM),
                  row_spec, row_spec, row_spec,
                  pl.BlockSpec(memory_space=pl.ANY),
                  pl.BlockSpec(memory_space=pl.ANY)],
        out_specs=pl.BlockSpec((None, n_heads, dh), lambda b, pt, sl: (b, 0, 0)),
        scratch_shapes=[pltpu.VMEM((n_heads, n_sub, page, dh), F32),
                        pltpu.VMEM((n_heads, n_sub, page, dh), F32),
                        pltpu.SemaphoreType.DMA((2, n_heads))],
    )
    out = pl.pallas_call(
        functools.partial(_sample_attend_kernel, layer=layer, n_heads=n_heads, n_pages=n_pages,
                          pages_per_block=ppb, scale=dh ** -0.5),
        grid_spec=grid_spec,
        out_shape=jax.ShapeDtypeStruct((bs, n_heads, dh), F32),
        compiler_params=_params(("arbitrary",)),
        name="sample_attend",
    )(page_table.reshape(-1), sel.reshape(-1), slopes,
      q_s.reshape(bs, n_heads, dh), k_new.reshape(bs, n_heads, dh), v_new.reshape(bs, n_heads, dh),
      cache_k, cache_v)
    return out.reshape(bs, hd)


def _router_kernel(x_ref, w_ref, b_ref, slab_ref, cnt_ref, carry_ref, *, n_tokens):
    i = pl.program_id(0)
    tm = x_ref.shape[0]

    @pl.when(i == 0)
    def _():
        carry_ref[...] = jnp.zeros_like(carry_ref)

    logits = jnp.dot(x_ref[...], w_ref[...], precision=lax.Precision.HIGHEST,
                     preferred_element_type=F32) + b_ref[...]
    lane = lax.broadcasted_iota(jnp.int32, logits.shape, 1)
    lane_f = lane.astype(F32)
    valid = (lax.broadcasted_iota(jnp.int32, (tm, 1), 0) + i * tm) < n_tokens
    vals, onehots = [], []
    work = logits
    for _ in range(TOP_K):
        mx = jnp.max(work, axis=-1, keepdims=True)
        idx = jnp.min(jnp.where(work == mx, lane_f, float(LANES)), axis=-1, keepdims=True)
        oh = lane_f == idx
        vals.append(mx)
        onehots.append(jnp.where(oh & valid, 1.0, 0.0))
        work = jnp.where(oh, -3e38, work)
    exps = [jnp.exp(v - vals[0]) for v in vals]
    denom = exps[0]
    for e in exps[1:]:
        denom = denom + e
    oh_all = onehots[0]
    for oh in onehots[1:]:
        oh_all = oh_all + oh
    r = lax.broadcasted_iota(jnp.int32, (tm, tm), 0)
    c = lax.broadcasted_iota(jnp.int32, (tm, tm), 1)
    tri = jnp.where(c < r, 1.0, 0.0).astype(BF16)
    before = jnp.dot(tri, oh_all.astype(BF16), preferred_element_type=F32) + carry_ref[...]
    slab = jnp.zeros(logits.shape, F32)
    for kk in range(TOP_K):
        idx_k = jnp.sum(onehots[kk] * lane_f, axis=-1, keepdims=True)
        rank_k = jnp.sum(onehots[kk] * before, axis=-1, keepdims=True)
        gate_k = jnp.where(valid, exps[kk] / denom, 0.0)
        slab = jnp.where(lane == kk, idx_k, slab)
        slab = jnp.where(lane == TOP_K + kk, gate_k, slab)
        slab = jnp.where(lane == 2 * TOP_K + kk, rank_k, slab)
    slab_ref[...] = slab
    carry_ref[...] += jnp.sum(oh_all, axis=0, keepdims=True)
    cnt_ref[...] = jnp.broadcast_to(carry_ref[...], cnt_ref.shape)


def _router(x, w, b, n_tokens):
    t, d = x.shape
    n_exp = w.shape[1]
    assert TOP_K <= n_exp <= LANES and 3 * TOP_K <= LANES
    w_pad = jnp.zeros((d, LANES), F32).at[:, :n_exp].set(w)
    b_pad = jnp.full((1, LANES), NEG_BIG, F32).at[0, :n_exp].set(b)
    return pl.pallas_call(
        functools.partial(_router_kernel, n_tokens=n_tokens),
        grid=(t // ROW_TILE,),
        in_specs=[pl.BlockSpec((ROW_TILE, d), lambda i: (i, 0)),
                  pl.BlockSpec((d, LANES), lambda i: (0, 0)),
                  pl.BlockSpec((1, LANES), lambda i: (0, 0))],
        out_specs=[pl.BlockSpec((ROW_TILE, LANES), lambda i: (i, 0)),
                   pl.BlockSpec((8, LANES), lambda i: (0, 0))],
        out_shape=[jax.ShapeDtypeStruct((t, LANES), F32), jax.ShapeDtypeStruct((8, LANES), F32)],
        scratch_shapes=[pltpu.VMEM((1, LANES), F32)],
        compiler_params=_params(("arbitrary",)),
        name="router",
    )(x, w_pad, b_pad)


def _expert_kernel(we_ref, ws_ref, wn_ref, tok_ref, dst_ref, x_hbm, wg_ref, wu_ref, wd_ref, bg_ref, bu_ref, bd_ref,
                   y_hbm, xbuf, xb_ref, acc_ref, gsem, ssem, *, n_chunks):
    del we_ref
    w = pl.program_id(0)
    c = pl.program_id(1)
    n_items = pl.num_programs(0)
    per_step = SUPER_TILE // n_chunks
    n_rows = wn_ref[w]
    slot = w % 2
    nxt = jnp.minimum(w + 1, n_items - 1)
    n_next = jnp.where(w + 1 < n_items, wn_ref[nxt], 0)
    prv = jnp.maximum(w - 1, 0)
    n_prev = jnp.where(w >= 1, wn_ref[prv], 0)
    n_prev2 = jnp.where(w >= 2, wn_ref[jnp.maximum(w - 2, 0)], 0)

    def gather_copy(r, tok):
        return pltpu.make_async_copy(x_hbm.at[pl.ds(tok, 1), :], xbuf.at[pl.ds(r, 1), :], gsem)

    def scatter_copy(sl, r, dst):
        return pltpu.make_async_copy(acc_ref.at[sl, pl.ds(r, 1), :], y_hbm.at[pl.ds(dst, 1), :], ssem.at[sl])

    def for_rows(lo, hi, body):
        groups = (hi - lo) // 8

        def step8(i, carry):
            for u in range(8):
                body(lo + i * 8 + u)
            return carry

        def step1(r, carry):
            body(r)
            return carry

        lax.fori_loop(0, groups, step8, 0)
        lax.fori_loop(lo + groups * 8, hi, step1, 0)

    @pl.when(c == 0)
    def _():
        @pl.when(w == 0)
        def _():
            xbuf[...] = jnp.zeros_like(xbuf)
            for_rows(0, n_rows, lambda r: gather_copy(r, tok_ref[ws_ref[0] + r]).start())

        for_rows(0, n_rows, lambda r: gather_copy(r, 0).wait())
        for_rows(0, n_prev2, lambda r: scatter_copy(slot, r, 0).wait())
        xb_ref[...] = xbuf[...].astype(BF16)
        acc_ref[slot] = jnp.broadcast_to(bd_ref[...], acc_ref.shape[1:])

    n_sub = (n_rows + SUB_TILE - 1) // SUB_TILE
    half = acc_ref.shape[2] // 2
    for k in range(1, SUPER_TILE // SUB_TILE + 1):
        rows = slice(0, k * SUB_TILE)

        @pl.when(n_sub == k)
        def _():
            xb = xb_ref[rows, :]
            gate = jnp.dot(xb, wg_ref[...].astype(BF16), preferred_element_type=F32) + bg_ref[...]
            up = jnp.dot(xb, wu_ref[...].astype(BF16), preferred_element_type=F32) + bu_ref[...]
            gate = jnp.minimum(gate, SWIGLU_LIMIT)
            up = jnp.clip(up, -SWIGLU_LIMIT, SWIGLU_LIMIT)
            act = (gate * (1.0 / (1.0 + jnp.exp(-SWIGLU_ALPHA * gate))) * (up + 1.0)).astype(BF16)
            for cols in (slice(0, half), slice(half, 2 * half)):
                acc_ref[slot, rows, cols] += jnp.dot(act, wd_ref[:, cols].astype(BF16), preferred_element_type=F32)

    first = c * per_step
    base_next = ws_ref[nxt]
    base_prev = ws_ref[prv]
    for_rows(jnp.minimum(first, n_next), jnp.minimum(first + per_step, n_next),
             lambda r: gather_copy(r, tok_ref[base_next + r]).start())
    for_rows(jnp.minimum(first, n_prev), jnp.minimum(first + per_step, n_prev),
             lambda r: scatter_copy(1 - slot, r, dst_ref[base_prev + r]).start())

    @pl.when((w == n_items - 1) & (c == n_chunks - 1))
    def _():
        for_rows(0, n_prev, lambda r: scatter_copy(1 - slot, r, 0).wait())


def _experts(x, item_expert, item_start, item_rows, tok, dst, w_gate_up, b_gate_up, w_down, b_down, layer):
    t, d = x.shape
    n_exp, _, two_de = w_gate_up.shape[1:]
    de = two_de // 2
    tn = _pick_tile(de, 256, LANES)
    nc = de // tn
    n_items = item_expert.shape[0]
    assert SUPER_TILE % nc == 0 and n_items >= 2
    bgu = b_gate_up.reshape(b_gate_up.shape[0], n_exp, 1, two_de)
    bdn = b_down.reshape(b_down.shape[0], n_exp, 1, d)

    def chunk(i, c, wn):
        return jnp.where(wn[i] > 0, c, nc - 1)

    def wspec(shape, index):
        return pl.BlockSpec(shape, lambda i, c, we, ws, wn, tk, ds: index(we[i], chunk(i, c, wn)))

    grid_spec = pltpu.PrefetchScalarGridSpec(
        num_scalar_prefetch=5,
        grid=(n_items, nc),
        in_specs=[pl.BlockSpec(memory_space=pl.ANY),
                  wspec((None, None, d, tn), lambda e, c: (layer, e, 0, c)),
                  wspec((None, None, d, tn), lambda e, c: (layer, e, 0, nc + c)),
                  wspec((None, None, tn, d), lambda e, c: (layer, e, c, 0)),
                  wspec((None, None, 1, tn), lambda e, c: (layer, e, 0, c)),
                  wspec((None, None, 1, tn), lambda e, c: (layer, e, 0, nc + c)),
                  wspec((None, None, 1, d), lambda e, c: (layer, e, 0, 0))],
        out_specs=pl.BlockSpec(memory_space=pl.ANY),
        scratch_shapes=[pltpu.VMEM((SUPER_TILE, d), F32),
                        pltpu.VMEM((SUPER_TILE, d), BF16),
                        pltpu.VMEM((2, SUPER_TILE, d), F32),
                        pltpu.SemaphoreType.DMA(()),
                        pltpu.SemaphoreType.DMA((2,))],
    )
    return pl.pallas_call(
        functools.partial(_expert_kernel, n_chunks=nc),
        grid_spec=grid_spec,
        out_shape=jax.ShapeDtypeStruct((TOP_K * t, d), F32),
        compiler_params=_params(("arbitrary", "arbitrary")),
        name="experts",
    )(item_expert, item_start, item_rows, tok, dst, x, w_gate_up, w_gate_up, w_down, bgu, bgu, bdn)


def _sample_expert_kernel(el_ref, nu_ref, x_ref, comb_ref, wg_ref, wu_ref, wd_ref, bg_ref, bu_ref, bd_ref,
                          g_ref, b_ref, o_ref, acc_ref, *, alpha):
    del el_ref
    j = pl.program_id(0)
    c = pl.program_id(1)

    @pl.when((j == 0) & (c == 0))
    def _():
        acc_ref[...] = jnp.zeros_like(acc_ref)

    @pl.when(j < nu_ref[0])
    def _():
        x = x_ref[...]
        gate = jnp.minimum(_dot3(x, wg_ref[...]) + bg_ref[...], SWIGLU_LIMIT)
        up = jnp.clip(_dot3(x, wu_ref[...]) + bu_ref[...], -SWIGLU_LIMIT, SWIGLU_LIMIT)
        act = gate * (1.0 / (1.0 + jnp.exp(-SWIGLU_ALPHA * gate))) * (up + 1.0)
        part = _dot3(act, wd_ref[...])
        part = part + jnp.where(c == 0, 1.0, 0.0) * bd_ref[...]
        acc_ref[...] += comb_ref[:, 0:1] * part

    @pl.when((j == pl.num_programs(0) - 1) & (c == pl.num_programs(1) - 1))
    def _():
        o_ref[...] = _layer_norm(alpha * x_ref[...] + acc_ref[...], g_ref[...], b_ref[...])


def _sample_moe_layer(x, layer, router_w, router_b, w_gate_up, b_gate_up, w_down, b_down, ln_g, ln_b, alpha):
    m, d = x.shape
    n_exp = router_w.shape[1]
    two_de = w_gate_up.shape[3]
    de = two_de // 2
    x_pad = jnp.zeros((ROW_TILE, d), F32).at[:m].set(x)
    slab, _ = _router(x_pad, router_w, router_b, m)
    idx = slab[:m, :TOP_K].astype(jnp.int32)
    gates = slab[:m, TOP_K:2 * TOP_K]
    comb = jnp.einsum("tk,tke->te", gates, jax.nn.one_hot(idx, n_exp, dtype=F32))
    used = jnp.any(comb > 0.0, axis=0)
    n_used = jnp.sum(used).astype(jnp.int32)
    order = jnp.argsort(jnp.logical_not(used), stable=True).astype(jnp.int32)
    n_slots = min(n_exp, m * TOP_K)
    slots = jnp.minimum(jnp.arange(n_slots, dtype=jnp.int32), n_used - 1)
    e_list = order[slots]
    comb_b = jnp.broadcast_to(comb.T[:, :, None], (n_exp, m, LANES))
    tn = _pick_tile(de, 256, LANES)
    nc = de // tn
    bgu = b_gate_up.reshape(b_gate_up.shape[0], n_exp, 1, two_de)
    bdn = b_down.reshape(b_down.shape[0], n_exp, 1, d)

    def chunk(j, c, nu):
        return jnp.where(j < nu[0], c, nc - 1)

    row = pl.BlockSpec((1, d), lambda j, c, el, nu: (0, 0))
    grid_spec = pltpu.PrefetchScalarGridSpec(
        num_scalar_prefetch=2,
        grid=(n_slots, nc),
        in_specs=[pl.BlockSpec((m, d), lambda j, c, el, nu: (0, 0)),
                  pl.BlockSpec((None, m, LANES), lambda j, c, el, nu: (el[j], 0, 0)),
                  pl.BlockSpec((None, None, d, tn), lambda j, c, el, nu: (layer, el[j], 0, chunk(j, c, nu))),
                  pl.BlockSpec((None, None, d, tn), lambda j, c, el, nu: (layer, el[j], 0, nc + chunk(j, c, nu))),
                  pl.BlockSpec((None, None, tn, d), lambda j, c, el, nu: (layer, el[j], chunk(j, c, nu), 0)),
                  pl.BlockSpec((None, None, 1, tn), lambda j, c, el, nu: (layer, el[j], 0, chunk(j, c, nu))),
                  pl.BlockSpec((None, None, 1, tn), lambda j, c, el, nu: (layer, el[j], 0, nc + chunk(j, c, nu))),
                  pl.BlockSpec((None, None, 1, d), lambda j, c, el, nu: (layer, el[j], 0, 0)),
                  row, row],
        out_specs=pl.BlockSpec((m, d), lambda j, c, el, nu: (0, 0)),
        scratch_shapes=[pltpu.VMEM((m, d), F32)],
    )
    return pl.pallas_call(
        functools.partial(_sample_expert_kernel, alpha=alpha),
        grid_spec=grid_spec,
        out_shape=jax.ShapeDtypeStruct((m, d), F32),
        compiler_params=_params(("arbitrary", "arbitrary")),
        name="sample_experts",
    )(e_list, n_used.reshape(1), x, comb_b, w_gate_up, w_gate_up, w_down, bgu, bgu, bdn,
      ln_g.reshape(1, d), ln_b.reshape(1, d))


def _combine_ln_kernel(*refs, alpha):
    y_refs = refs[:TOP_K]
    slab_ref, x_ref, g_ref, b_ref, o_ref = refs[TOP_K:]
    f = alpha * x_ref[...]
    for kk in range(TOP_K):
        f = f + slab_ref[:, TOP_K + kk:TOP_K + kk + 1] * y_refs[kk][...]
    o_ref[...] = _layer_norm(f, g_ref[...], b_ref[...])


def _combine_ln(y4, slab, x, g, b, alpha):
    t, d = x.shape
    nt = t // ROW_TILE

    def y_spec(kk):
        return pl.BlockSpec((ROW_TILE, d), lambda i: (kk * nt + i, 0))

    return pl.pallas_call(
        functools.partial(_combine_ln_kernel, alpha=alpha),
        grid=(nt,),
        in_specs=[y_spec(kk) for kk in range(TOP_K)] + [
            pl.BlockSpec((ROW_TILE, LANES), lambda i: (i, 0)),
            pl.BlockSpec((ROW_TILE, d), lambda i: (i, 0)),
            pl.BlockSpec((1, d), lambda i: (0, 0)),
            pl.BlockSpec((1, d), lambda i: (0, 0))],
        out_specs=pl.BlockSpec((ROW_TILE, d), lambda i: (i, 0)),
        out_shape=jax.ShapeDtypeStruct((t, d), F32),
        compiler_params=_params(("arbitrary",)),
        name="combine_ln",
    )(*([y4] * TOP_K), slab, x, g.reshape(1, d), b.reshape(1, d))


def _moe_layer(x, layer, router_w, router_b, w_gate_up, b_gate_up, w_down, b_down, ln_g, ln_b, alpha):
    t, d = x.shape
    n_exp = router_w.shape[1]
    slab, cnt = _router(x, router_w, router_b, t)
    idx = slab[:, :TOP_K].astype(jnp.int32)
    rank = slab[:, 2 * TOP_K:3 * TOP_K].astype(jnp.int32)
    counts = cnt[0, :n_exp].astype(jnp.int32)
    first_row = jnp.cumsum(counts) - counts
    pos = first_row[idx] + rank
    pair_id = jnp.arange(t * TOP_K, dtype=jnp.int32).reshape(t, TOP_K)
    pairs = jnp.zeros((t * TOP_K,), jnp.int32).at[pos.reshape(-1)].set(pair_id.reshape(-1))
    tok = pairs // TOP_K
    dst = (pairs % TOP_K) * t + tok
    n_items = (t * TOP_K) // SUPER_TILE + n_exp
    items_per_exp = (counts + SUPER_TILE - 1) // SUPER_TILE
    item_end = jnp.cumsum(items_per_exp)
    ids = jnp.arange(n_items, dtype=jnp.int32)
    live = ids < item_end[-1]
    item_expert = jnp.minimum(jnp.searchsorted(item_end, jnp.minimum(ids, item_end[-1] - 1), side="right"),
                              n_exp - 1).astype(jnp.int32)
    local = ids - (item_end - items_per_exp)[item_expert]
    item_start = jnp.where(live, first_row[item_expert] + local * SUPER_TILE, 0).astype(jnp.int32)
    item_rows = jnp.where(live, jnp.clip(counts[item_expert] - local * SUPER_TILE, 0, SUPER_TILE), 0).astype(jnp.int32)
    y4 = _experts(x, item_expert, item_start, item_rows, tok, dst, w_gate_up, b_gate_up, w_down, b_down, layer)
    return _combine_ln(y4, slab, x, ln_g, ln_b, alpha)


def _gmlp_prompt(x, w_in, b_in, lnv_g, lnv_b, w_s, b_s, w_out, ln_g, ln_b, alpha):
    z = _mm_bias(x, w_in, b_in, gelu=True)
    causal = jnp.tril(jnp.ones((CHUNK, CHUNK), dtype=bool))
    vn, gated = _spatial_mix(z, jnp.where(causal[None], w_s, 0.0), b_s.T, lnv_g, lnv_b)
    return _mm_resid_ln(gated, w_out, x, ln_g, ln_b, alpha), vn


def _gmlp_sample(x, w_in, b_in, lnv_g, lnv_b, w_s, b_s, w_out, ln_g, ln_b, alpha):
    z = _mm_bias(x, w_in, b_in, gelu=True, hi=True)
    group_dim = lnv_g.shape[0] // w_s.shape[0]
    vn, gated = _sample_gate(z, jnp.repeat(w_s[:, 0, 0], group_dim), jnp.repeat(b_s[:, 0], group_dim), lnv_g, lnv_b)
    return _mm_resid_ln(gated, w_out, x, ln_g, ln_b, alpha, hi=True), vn


def kernel(x_prompt, x_sample, cache_k, cache_v, page_table, w_in_a, b_in_a, ln_v_g, ln_v_b, w_s, b_s, w_out_a, w_qkv, w_o, ln_mix_g, ln_mix_b, ln_ffn_g, ln_ffn_b, router_w, router_b, w_gate_up, b_gate_up, w_down, b_down):
    bp, sp, d = x_prompt.shape
    bs, ss, _ = x_sample.shape
    depth = ln_mix_g.shape[0]
    n_heads = cache_k.shape[3]
    dh = cache_k.shape[4]
    page = cache_k.shape[2]
    n_pages = page_table.shape[1]
    assert ss == 1 and sp % CHUNK == 0 and sp % MOBA_BLOCK == 0
    assert MOBA_BLOCK % page == 0 and (n_pages * page) % MOBA_BLOCK == 0
    alpha = (2 * depth) ** 0.25
    n_prompt = bp * sp
    n_tokens = n_prompt + bs
    slopes = 2.0 ** (-8.0 * jnp.arange(1, n_heads + 1, dtype=F32) / n_heads)
    no_bias = jnp.zeros((3 * d,), F32)

    xp = x_prompt.reshape(n_prompt, d)
    xs = x_sample.reshape(bs, d)
    v_rows_p, v_rows_s, k_p, v_p, k_s, v_s = [], [], [], [], [], []
    for i in range(depth):
        li = i // 2
        g_mix, b_mix = ln_mix_g[i], ln_mix_b[i]
        if i % 2 == 0:
            mixer = (w_in_a[li], b_in_a[li], ln_v_g[li], ln_v_b[li], w_s[li], b_s[li], w_out_a[li], g_mix, b_mix, alpha)
            xp, vn_p = _gmlp_prompt(xp, *mixer)
            xs, vn_s = _gmlp_sample(xs, *mixer)
            last0 = ((sp - 1) // CHUNK) * CHUNK
            v_rows_p.append(vn_p.reshape(bp, sp, -1)[:, last0:])
            v_rows_s.append(vn_s.reshape(bs, ss, -1))
        else:
            qkv = _mm_bias(xp, w_qkv[li], no_bias, gelu=False)
            attn_p = _moba_prompt(qkv, slopes, bp, sp, n_heads)
            xp = _mm_resid_ln(attn_p, w_o[li], xp, g_mix, b_mix, alpha)
            k_p.append(qkv[:, d:2 * d].reshape(bp, sp, n_heads, dh))
            v_p.append(qkv[:, 2 * d:].reshape(bp, sp, n_heads, dh))

            qkv_s = _mm_bias(xs, w_qkv[li], no_bias, gelu=False, hi=True)
            q_s, k_new, v_new = qkv_s[:, :d], qkv_s[:, d:2 * d], qkv_s[:, 2 * d:]
            ksum = _past_block_sums(cache_k, li, page_table)
            sel = _sample_select(q_s, ksum, n_heads)
            attn_s = _sample_attend(q_s, k_new, v_new, cache_k, cache_v, li, page_table, sel, slopes, n_heads)
            xs = _mm_resid_ln(attn_s, w_o[li], xs, g_mix, b_mix, alpha, hi=True)
            k_s.append(k_new.reshape(bs, ss, n_heads, dh))
            v_s.append(v_new.reshape(bs, ss, n_heads, dh))
        moe = (router_w[i], router_b[i], w_gate_up, b_gate_up, w_down, b_down, ln_ffn_g[i], ln_ffn_b[i], alpha)
        if i < depth - 1:
            xp = _moe_layer(xp, i, *moe)
            xs = _sample_moe_layer(xs, i, *moe)
        else:
            t = _round_up(n_tokens, ROW_TILE)
            x = _moe_layer(jnp.concatenate([xp, xs, jnp.zeros((t - n_tokens, d), F32)], axis=0), i, *moe)
            xp, xs = x[:n_prompt], x[n_prompt:n_tokens]
    return (xp.reshape(bp, sp, d), xs.reshape(bs, ss, d), jnp.stack(v_rows_p), jnp.stack(v_rows_s),
            jnp.stack(k_p), jnp.stack(v_p), jnp.stack(k_s), jnp.stack(v_s))
```
